```python
import math
import jax
import jax.numpy as jnp
from jax import lax
import numpy as np

D_MODEL = 2048
BATCH = 4
SEQ = 2048
DEPTH = 4
DEC_BATCH = 32
DEC_SEQ = 1
PAST_LEN = 16384
PAGE_SIZE = 128

HEAD_DIM = 64
N_HEADS = 32
KV_HEADS_A = 4
WINDOW_A = 128
B_PATTERNS = ((128, 1), (512, 4), (2048, 16))
N_GROUPS_B = 3
KV_HEADS_B = 4
N_BUCKETS = 32
MAX_DISTANCE = 2048
N_EXPERTS = 32
TOP_K = 4
D_FF_EXPERT = 2048
SWIGLU_ALPHA = 1.702
SWIGLU_LIMIT = 7.0
ATTN_BLOCK = 128
MOE_BLOCK = 128
NORM_EPS = 1e-6
NEG_INF = -1e30
ATTN_WIDTH = N_HEADS * HEAD_DIM
QKV_A = (N_HEADS + 2 * KV_HEADS_A) * HEAD_DIM
QKV_B = N_GROUPS_B * (N_HEADS + 2 * KV_HEADS_B) * HEAD_DIM

kernel_name = 'hybrid_swa_sink_dilated_moe_step'


def rms_norm(x, gain):
    xf = x.astype(jnp.float32)
    y = xf * lax.rsqrt(jnp.mean(xf * xf, axis=-1, keepdims=True) + NORM_EPS)
    return y.astype(x.dtype) * gain


def modulate(h, shift, scale):
    return h * (1 + scale[:, None]) + shift[:, None]


def t5_bucket(dist):
    dist = jnp.maximum(dist, 0)
    max_exact = N_BUCKETS // 2
    d = dist.astype(jnp.float32)
    ratio = jnp.log(jnp.maximum(d, 1.0) / max_exact) / math.log(MAX_DISTANCE / max_exact)
    large = max_exact + (ratio * (N_BUCKETS - max_exact)).astype(jnp.int32)
    large = jnp.minimum(large, N_BUCKETS - 1)
    return jnp.where(dist < max_exact, dist, large)


def masked_softmax(s, valid, sink):
    s = jnp.where(valid, s, NEG_INF)
    m = jnp.max(s, axis=-1, keepdims=True)
    if sink is not None:
        m = jnp.maximum(m, sink)
    p = jnp.exp(s - m)
    denom = jnp.sum(p, axis=-1, keepdims=True)
    if sink is not None:
        denom = denom + jnp.exp(sink - m)
    return p / denom, (m + jnp.log(denom))[..., 0]


def banded_window_attention(q, k, v, window, dilation, rel_bias, sinks):
    n, L, h, hd = q.shape
    kvh = k.shape[2]
    g = h // kvh
    blk = min(ATTN_BLOCK, L)
    nb = -(-L // blk)
    pad = nb * blk - L
    if pad:
        cfg = ((0, 0), (0, pad), (0, 0), (0, 0))
        q, k, v = jnp.pad(q, cfg), jnp.pad(k, cfg), jnp.pad(v, cfg)
    qb = q.reshape(n, nb, blk, kvh, g, hd)
    kb = k.reshape(n, nb, blk, kvh, hd)
    vb = v.reshape(n, nb, blk, kvh, hd)
    prev = ((0, 0), (1, 0), (0, 0), (0, 0), (0, 0))
    kw = jnp.concatenate([jnp.pad(kb[:, :-1], prev), kb], axis=2)
    vw = jnp.concatenate([jnp.pad(vb[:, :-1], prev), vb], axis=2)
    s = jnp.einsum('nbqhgd,nbkhd->nbhgqk', qb, kw).astype(jnp.float32) * hd ** -0.5
    qi = jnp.arange(blk)[:, None]
    kj = jnp.arange(2 * blk)[None, :]
    dist = blk + qi - kj
    bias = rel_bias[t5_bucket(dist * dilation)].astype(jnp.float32)
    bias = jnp.transpose(bias, (2, 0, 1)).reshape(kvh, g, blk, 2 * blk)
    in_window = (dist >= 0) & (dist <= window)
    has_prev = (jnp.arange(nb) > 0)[:, None, None] | (kj >= blk)[None]
    valid = (in_window[None] & has_prev)[None, :, None, None]
    sink = None if sinks is None else sinks.astype(jnp.float32).reshape(kvh, g, 1, 1)
    p, lse = masked_softmax(s + bias, valid, sink)
    o = jnp.einsum('nbhgqk,nbkhd->nbqhgd', p.astype(v.dtype), vw)
    o = o.reshape(n, nb * blk, h, hd)[:, :L]
    lse = jnp.transpose(lse, (0, 1, 4, 2, 3)).reshape(n, nb * blk, h)[:, :L]
    return o, lse


def window_decode_attention(q, k_all, v_all, window, dilation, rel_bias, sinks):
    bd, t, h, hd = q.shape
    kvh = k_all.shape[2]
    g = h // kvh
    n_buf = k_all.shape[1] - t
    steps = jnp.arange(window // dilation + 1)
    idx = n_buf + jnp.arange(t)[:, None] - dilation * steps[None, :]
    valid = idx >= 0
    idx = jnp.maximum(idx, 0)
    kg = jnp.take(k_all, idx, axis=1)
    vg = jnp.take(v_all, idx, axis=1)
    qg = q.reshape(bd, t, kvh, g, hd)
    s = jnp.einsum('bthgd,btjhd->bhgtj', qg, kg).astype(jnp.float32) * hd ** -0.5
    bias = rel_bias[t5_bucket(dilation * steps)].astype(jnp.float32)
    bias = bias.T.reshape(kvh, g, 1, steps.shape[0])
    sink = None if sinks is None else sinks.astype(jnp.float32).reshape(kvh, g, 1, 1)
    p, lse = masked_softmax(s + bias, valid[None, None, None], sink)
    o = jnp.einsum('bhgtj,btjhd->bthgd', p.astype(v_all.dtype), vg).reshape(bd, t, h, hd)
    lse = jnp.transpose(lse, (0, 3, 1, 2)).reshape(bd, t, h)
    return o, lse


def to_residues(x, d):
    b, s = x.shape[:2]
    rest = x.shape[2:]
    return jnp.swapaxes(x.reshape(b, s // d, d, *rest), 1, 2).reshape(b * d, s // d, *rest)


def from_residues(x, b, d):
    l = x.shape[1]
    rest = x.shape[2:]
    return jnp.swapaxes(x.reshape(b, d, l, *rest), 1, 2).reshape(b, l * d, *rest)


def combine_groups(outs, lses):
    w = jax.nn.softmax(jnp.stack(lses, axis=0), axis=0)
    return jnp.sum(w[..., None] * jnp.stack(outs, axis=0), axis=0)


def split_qkv_a(h, w_qkv, qk_gain):
    b, s, _ = h.shape
    qkv = h @ w_qkv
    nq = N_HEADS * HEAD_DIM
    nk = KV_HEADS_A * HEAD_DIM
    q = qkv[..., :nq].reshape(b, s, N_HEADS, HEAD_DIM)
    k = qkv[..., nq:nq + nk].reshape(b, s, KV_HEADS_A, HEAD_DIM)
    v = qkv[..., nq + nk:].reshape(b, s, KV_HEADS_A, HEAD_DIM)
    return rms_norm(q, qk_gain[0]), rms_norm(k, qk_gain[1]), v


def split_qkv_b(h, w_qkv, qk_gain):
    b, s, _ = h.shape
    qkv = h @ w_qkv
    nq = N_GROUPS_B * N_HEADS * HEAD_DIM
    nk = N_GROUPS_B * KV_HEADS_B * HEAD_DIM
    q = qkv[..., :nq].reshape(b, s, N_GROUPS_B, N_HEADS, HEAD_DIM)
    k = qkv[..., nq:nq + nk].reshape(b, s, N_GROUPS_B, KV_HEADS_B, HEAD_DIM)
    v = qkv[..., nq + nk:].reshape(b, s, N_GROUPS_B, KV_HEADS_B, HEAD_DIM)
    return rms_norm(q, qk_gain[0]), rms_norm(k, qk_gain[1]), v


def mixer_a_prompt(h, w_qkv, qk_gain, sinks, w_o, rel_bias):
    b, s, _ = h.shape
    q, k, v = split_qkv_a(h, w_qkv, qk_gain)
    o, _ = banded_window_attention(q, k, v, WINDOW_A, 1, rel_bias, sinks)
    w = min(WINDOW_A, s)
    state = jnp.stack([k[:, s - w:], v[:, s - w:]], axis=1)
    return o.reshape(b, s, ATTN_WIDTH) @ w_o, state


def mixer_a_sample(h, cache, w_qkv, qk_gain, sinks, w_o, rel_bias):
    bd, t, _ = h.shape
    q, k, v = split_qkv_a(h, w_qkv, qk_gain)
    k_all = jnp.concatenate([cache[:, 0].astype(k.dtype), k], axis=1)
    v_all = jnp.concatenate([cache[:, 1].astype(v.dtype), v], axis=1)
    o, _ = window_decode_attention(q, k_all, v_all, WINDOW_A, 1, rel_bias, sinks)
    w = cache.shape[2]
    state = jnp.stack([k_all[:, -w:], v_all[:, -w:]], axis=1)
    return o.reshape(bd, t, ATTN_WIDTH) @ w_o, state


def mixer_b_prompt(h, w_qkv, qk_gain, w_o, rel_bias):
    b, s, _ = h.shape
    q, k, v = split_qkv_b(h, w_qkv, qk_gain)
    outs, lses, states = [], [], []
    for gi, (win, dil) in enumerate(B_PATTERNS):
        o, lse = banded_window_attention(to_residues(q[:, :, gi], dil), to_residues(k[:, :, gi], dil),
                                         to_residues(v[:, :, gi], dil), win // dil, dil, rel_bias, None)
        outs.append(from_residues(o, b, dil))
        lses.append(from_residues(lse, b, dil))
        w = min(win, s)
        states.append(jnp.stack([k[:, s - w:, gi], v[:, s - w:, gi]], axis=1))
    o = combine_groups(outs, lses).astype(h.dtype)
    return o.reshape(b, s, ATTN_WIDTH) @ w_o, states


def mixer_b_sample(h, caches, w_qkv, qk_gain, w_o, rel_bias):
    bd, t, _ = h.shape
    q, k, v = split_qkv_b(h, w_qkv, qk_gain)
    outs, lses, states = [], [], []
    for gi, (win, dil) in enumerate(B_PATTERNS):
        cache = caches[gi]
        k_all = jnp.concatenate([cache[:, 0].astype(k.dtype), k[:, :, gi]], axis=1)
        v_all = jnp.concatenate([cache[:, 1].astype(v.dtype), v[:, :, gi]], axis=1)
        o, lse = window_decode_attention(q[:, :, gi], k_all, v_all, win, dil, rel_bias, None)
        outs.append(o)
        lses.append(lse)
        w = cache.shape[2]
        states.append(jnp.stack([k_all[:, -w:], v_all[:, -w:]], axis=1))
    o = combine_groups(outs, lses).astype(h.dtype)
    return o.reshape(bd, t, ATTN_WIDTH) @ w_o, states


def clamped_swiglu(hh):
    x_glu, x_lin = jnp.split(hh, 2, axis=-1)
    x_glu = jnp.minimum(x_glu, SWIGLU_LIMIT)
    x_lin = jnp.clip(x_lin, -SWIGLU_LIMIT, SWIGLU_LIMIT)
    return x_glu * jax.nn.sigmoid(SWIGLU_ALPHA * x_glu) * (x_lin + 1)


def moe_ffn(x, w_router, b_router, w_up, b_up, w_down, b_down):
    n, d = x.shape
    logits = (x @ w_router + b_router).astype(jnp.float32)
    top_vals, top_idx = lax.top_k(logits, TOP_K)
    gates = jax.nn.softmax(top_vals, axis=-1).astype(x.dtype)
    nk = n * TOP_K
    e_flat = top_idx.reshape(nk)
    order = jnp.argsort(e_flat)
    e_sorted = e_flat[order]
    tok_sorted = order // TOP_K
    gate_sorted = gates.reshape(nk)[order]
    counts = jnp.zeros((N_EXPERTS,), jnp.int32).at[e_flat].add(1)
    padded = (counts + MOE_BLOCK - 1) // MOE_BLOCK * MOE_BLOCK
    start = jnp.cumsum(counts) - counts
    pend = jnp.cumsum(padded)
    pstart = pend - padded
    slot = pstart[e_sorted] + jnp.arange(nk) - start[e_sorted]
    n_blocks = (nk + N_EXPERTS * (MOE_BLOCK - 1) + MOE_BLOCK - 1) // MOE_BLOCK
    n_slots = n_blocks * MOE_BLOCK
    slot_tok = jnp.zeros((n_slots,), jnp.int32).at[slot].set(tok_sorted)
    slot_gate = jnp.zeros((n_slots,), x.dtype).at[slot].set(gate_sorted)
    block_expert = jnp.minimum(jnp.searchsorted(pend, jnp.arange(n_blocks) * MOE_BLOCK, side='right'),
                               N_EXPERTS - 1)

    def expert_block(args):
        tok, gate, e = args
        hh = x[tok] @ w_up[e] + b_up[e]
        return (clamped_swiglu(hh) @ w_down[e] + b_down[e]) * gate[:, None]

    ys = lax.map(expert_block, (slot_tok.reshape(n_blocks, MOE_BLOCK),
                                slot_gate.reshape(n_blocks, MOE_BLOCK), block_expert))
    return jnp.zeros_like(x).at[slot_tok].add(ys.reshape(n_slots, d))


def setup_inputs(seed: int = 0) -> dict:
    key = jax.random.key(seed)
    ks = jax.random.split(key, 25)
    n_a = (DEPTH + 1) // 2
    n_b = DEPTH // 2

    def nrm(k, shape, scale):
        return jax.random.normal(k, shape, jnp.float32) * scale

    def kv_cache(k, n_layers, n_kv, win):
        return nrm(k, (n_layers, DEC_BATCH, 2, min(win, PAST_LEN), n_kv, HEAD_DIM), 1.0)

    return {
        'x_prompt': nrm(ks[0], (BATCH, SEQ, D_MODEL), 1.0),
        'x_sample': nrm(ks[1], (DEC_BATCH, DEC_SEQ, D_MODEL), 1.0),
        'c_prompt': nrm(ks[2], (BATCH, D_MODEL), 1.0),
        'c_sample': nrm(ks[3], (DEC_BATCH, D_MODEL), 1.0),
        'cache_a_kv': kv_cache(ks[4], n_a, KV_HEADS_A, WINDOW_A),
        'cache_b1_kv': kv_cache(ks[5], n_b, KV_HEADS_B, B_PATTERNS[0][0]),
        'cache_b2_kv': kv_cache(ks[6], n_b, KV_HEADS_B, B_PATTERNS[1][0]),
        'cache_b3_kv': kv_cache(ks[7], n_b, KV_HEADS_B, B_PATTERNS[2][0]),
        'rel_bias': nrm(ks[8], (N_BUCKETS, N_HEADS), 0.2),
        'norm_gain': 1.0 + nrm(ks[9], (DEPTH, 2, D_MODEL), 0.02),
        'w_ada': nrm(ks[10], (DEPTH, D_MODEL, 6 * D_MODEL), 0.5 * D_MODEL ** -0.5),
        'b_ada': nrm(ks[11], (DEPTH, 6 * D_MODEL), 0.01),
        'w_qkv_a': nrm(ks[12], (n_a, D_MODEL, QKV_A), D_MODEL ** -0.5),
        'qk_gain_a': 1.0 + nrm(ks[13], (n_a, 2, HEAD_DIM), 0.02),
        'sinks_a': nrm(ks[14], (n_a, N_HEADS), 0.5),
        'w_o_a': nrm(ks[15], (n_a, ATTN_WIDTH, D_MODEL), ATTN_WIDTH ** -0.5),
        'w_qkv_b': nrm(ks[16], (n_b, D_MODEL, QKV_B), D_MODEL ** -0.5),
        'qk_gain_b': 1.0 + nrm(ks[17], (n_b, 2, HEAD_DIM), 0.02),
        'w_o_b': nrm(ks[18], (n_b, ATTN_WIDTH, D_MODEL), ATTN_WIDTH ** -0.5),
        'w_router': nrm(ks[19], (DEPTH, D_MODEL, N_EXPERTS), D_MODEL ** -0.5),
        'b_router': nrm(ks[20], (DEPTH, N_EXPERTS), 0.01),
        'w_up': nrm(ks[21], (DEPTH, N_EXPERTS, D_MODEL, 2 * D_FF_EXPERT), D_MODEL ** -0.5),
        'b_up': nrm(ks[22], (DEPTH, N_EXPERTS, 2 * D_FF_EXPERT), 0.01),
        'w_down': nrm(ks[23], (DEPTH, N_EXPERTS, D_FF_EXPERT, D_MODEL), D_FF_EXPERT ** -0.5),
        'b_down': nrm(ks[24], (DEPTH, N_EXPERTS, D_MODEL), 0.01),
    }


def reference(x_prompt, x_sample, c_prompt, c_sample, cache_a_kv, cache_b1_kv, cache_b2_kv, cache_b3_kv,
              rel_bias, norm_gain, w_ada, b_ada, w_qkv_a, qk_gain_a, sinks_a, w_o_a,
              w_qkv_b, qk_gain_b, w_o_b, w_router, b_router, w_up, b_up, w_down, b_down):
    b, s, d = x_prompt.shape
    bd, t, _ = x_sample.shape
    xp, xs = x_prompt, x_sample
    a_p, a_s = [], []
    b_p = [[], [], []]
    b_s = [[], [], []]
    caches_b = (cache_b1_kv, cache_b2_kv, cache_b3_kv)
    for i in range(DEPTH):
        mp = jnp.split(jax.nn.silu(c_prompt) @ w_ada[i] + b_ada[i], 6, axis=-1)
        ms = jnp.split(jax.nn.silu(c_sample) @ w_ada[i] + b_ada[i], 6, axis=-1)
        hp = modulate(rms_norm(xp, norm_gain[i, 0]), mp[0], mp[1])
        hs = modulate(rms_norm(xs, norm_gain[i, 0]), ms[0], ms[1])
        j = i // 2
        if i % 2 == 0:
            yp, st_p = mixer_a_prompt(hp, w_qkv_a[j], qk_gain_a[j], sinks_a[j], w_o_a[j], rel_bias)
            ys, st_s = mixer_a_sample(hs, cache_a_kv[j], w_qkv_a[j], qk_gain_a[j], sinks_a[j], w_o_a[j], rel_bias)
            a_p.append(st_p)
            a_s.append(st_s)
        else:
            yp, st_p = mixer_b_prompt(hp, w_qkv_b[j], qk_gain_b[j], w_o_b[j], rel_bias)
            ys, st_s = mixer_b_sample(hs, [c[j] for c in caches_b], w_qkv_b[j], qk_gain_b[j], w_o_b[j], rel_bias)
            for gi in range(N_GROUPS_B):
                b_p[gi].append(st_p[gi])
                b_s[gi].append(st_s[gi])
        xp = xp + mp[2][:, None] * yp
        xs = xs + ms[2][:, None] * ys
        hp = modulate(rms_norm(xp, norm_gain[i, 1]), mp[3], mp[4])
        hs = modulate(rms_norm(xs, norm_gain[i, 1]), ms[3], ms[4])
        f = moe_ffn(jnp.concatenate([hp.reshape(b * s, d), hs.reshape(bd * t, d)], axis=0),
                    w_router[i], b_router[i], w_up[i], b_up[i], w_down[i], b_down[i])
        xp = xp + mp[5][:, None] * f[:b * s].reshape(b, s, d)
        xs = xs + ms[5][:, None] * f[b * s:].reshape(bd, t, d)
    return (xp, xs,
            jnp.stack(a_p), jnp.stack(a_s),
            jnp.stack(b_p[0]), jnp.stack(b_s[0]),
            jnp.stack(b_p[1]), jnp.stack(b_s[1]),
            jnp.stack(b_p[2]), jnp.stack(b_s[2]))
```

```python
import functools
import math

import jax
import jax.numpy as jnp
from jax import lax
from jax.experimental import pallas as pl
from jax.experimental.pallas import tpu as pltpu

D_MODEL = 2048
DEPTH = 4
HEAD_DIM = 64
N_HEADS = 32
KV_HEADS = 4
WINDOW_A = 128
B_PATTERNS = ((128, 1), (512, 4), (2048, 16))
N_GROUPS_B = 3
N_BUCKETS = 32
MAX_DISTANCE = 2048
N_EXPERTS = 32
TOP_K = 4
D_FF = 2048
SWIGLU_ALPHA = 1.702
SWIGLU_LIMIT = 7.0
ATTN_BLOCK = 128
NORM_EPS = 1e-6
NEG_INF = -1e30
ATTN_WIDTH = N_HEADS * HEAD_DIM

F32 = jnp.float32
BF16 = jnp.bfloat16

VMEM_LIMIT_BYTES = 56 * 1024 * 1024

MOE_TM = 256
MOE_SB_TILES = 5
MOE_SB_ROWS = MOE_TM * MOE_SB_TILES
MOE_TF = 256
MOE_NF = D_FF // MOE_TF


def _moe_expert_kernel(sb_e_ref, sb_tile0_ref, sb_ntile_ref,
                       tok_ref,
                       x_hbm,
                       wug_ref, wul_ref, bug_ref, bul_ref, wd_ref, bd_ref,
                       ys_hbm,
                       xf_ref, xb_ref, yacc_ref, wub_ref, wdb_ref, gsem, osem):
    del sb_e_ref
    s = pl.program_id(0)
    f = pl.program_id(1)
    ntile = sb_ntile_ref[s]

    def row_copy(tok, r):
        return pltpu.make_async_copy(x_hbm.at[pl.ds(tok, 1)], xf_ref.at[pl.ds(r, 1)], gsem)

    def tile_wait_copy():
        return pltpu.make_async_copy(x_hbm.at[pl.ds(0, MOE_TM)], xf_ref.at[pl.ds(0, MOE_TM)], gsem)

    def out_copy(i):
        r0 = pl.multiple_of(i * MOE_TM, MOE_TM)
        g0 = pl.multiple_of((sb_tile0_ref[s] + i) * MOE_TM, MOE_TM)
        return pltpu.make_async_copy(yacc_ref.at[pl.ds(r0, MOE_TM)], ys_hbm.at[pl.ds(g0, MOE_TM)], osem)

    @pl.when(ntile > 0)
    def _active():
        @pl.when(f == 0)
        def _gather():
            def issue(r, c):
                row_copy(tok_ref[0, 0, r], r).start()
                return c
            lax.fori_loop(0, ntile * MOE_TM, issue, 0)

            def wait_cast(i, c):
                tile_wait_copy().wait()
                return c
            lax.fori_loop(0, ntile, wait_cast, 0)

            def cast(i, c):
                r0 = pl.multiple_of(i * MOE_TM, MOE_TM)
                xb_ref[pl.ds(r0, MOE_TM), :] = xf_ref[pl.ds(r0, MOE_TM), :].astype(BF16)
                return c
            lax.fori_loop(0, ntile, cast, 0)

        wub_ref[:, :MOE_TF] = wug_ref[...].astype(BF16)
        wub_ref[:, MOE_TF:] = wul_ref[...].astype(BF16)
        wdb_ref[...] = wd_ref[...].astype(BF16)

        def tile(i, c):
            r0 = pl.multiple_of(i * MOE_TM, MOE_TM)
            hh = jnp.dot(xb_ref[pl.ds(r0, MOE_TM), :], wub_ref[...], preferred_element_type=F32)
            hg = jnp.minimum(hh[:, :MOE_TF] + bug_ref[...], SWIGLU_LIMIT)
            hl = jnp.clip(hh[:, MOE_TF:] + bul_ref[...], -SWIGLU_LIMIT, SWIGLU_LIMIT)
            act = hg * (1.0 / (1.0 + jnp.exp(-SWIGLU_ALPHA * hg))) * (hl + 1.0)
            y = jnp.dot(act.astype(BF16), wdb_ref[...], preferred_element_type=F32)

            @pl.when(f == 0)
            def _():
                yacc_ref[pl.ds(r0, MOE_TM), :] = y + bd_ref[...]

            @pl.when(f > 0)
            def _():
                yacc_ref[pl.ds(r0, MOE_TM), :] += y
            return c
        lax.fori_loop(0, ntile, tile, 0)

        @pl.when(f == MOE_NF - 1)
        def _writeback():
            def start(i, c):
                out_copy(i).start()
                return c
            lax.fori_loop(0, ntile, start, 0)

            def wait(i, c):
                out_copy(i).wait()
                return c
            lax.fori_loop(0, ntile, wait, 0)


def _moe_n_tiles(nk):
    return (nk + N_EXPERTS * (MOE_TM - 1) + MOE_TM - 1) // MOE_TM


def _moe_n_superblocks(nk):
    return (_moe_n_tiles(nk) + N_EXPERTS * (MOE_SB_TILES - 1) + MOE_SB_TILES - 1) // MOE_SB_TILES


def _moe_plan(counts, nk):
    n_sb = _moe_n_superblocks(nk)
    tiles = (counts + MOE_TM - 1) // MOE_TM
    tile_end = jnp.cumsum(tiles)
    tile_start = tile_end - tiles
    nsb = (tiles + MOE_SB_TILES - 1) // MOE_SB_TILES
    sb_end = jnp.cumsum(nsb)
    sb_start = sb_end - nsb
    total_sb = sb_end[-1]
    s = jnp.arange(n_sb, dtype=jnp.int32)
    s_eff = jnp.minimum(s, total_sb - 1)
    e = jnp.minimum(jnp.searchsorted(sb_end, s_eff, side='right'), N_EXPERTS - 1).astype(jnp.int32)
    local = s_eff - sb_start[e]
    tile0 = tile_start[e] + local * MOE_SB_TILES
    ntile = jnp.clip(tiles[e] - local * MOE_SB_TILES, 0, MOE_SB_TILES)
    ntile = jnp.where(s < total_sb, ntile, 0)
    return tile_start.astype(jnp.int32), e, tile0.astype(jnp.int32), ntile.astype(jnp.int32)


def _moe_experts(x, slot_tok, sb_e, sb_tile0, sb_ntile, w_up, b_up, w_down, b_down):
    n_slots = slot_tok.shape[0]
    n_sb = sb_e.shape[0]
    win = sb_tile0[:, None] * MOE_TM + jnp.arange(MOE_SB_ROWS, dtype=jnp.int32)[None, :]
    sb_tok = slot_tok[jnp.minimum(win, n_slots - 1)].reshape(n_sb, 1, MOE_SB_ROWS)

    def ff(s, f, e_ref, t0_ref, nt_ref):
        return jnp.where(nt_ref[s] > 0, f, MOE_NF - 1)

    grid_spec = pltpu.PrefetchScalarGridSpec(
        num_scalar_prefetch=3,
        grid=(n_sb, MOE_NF),
        in_specs=[
            pl.BlockSpec((1, 1, MOE_SB_ROWS), lambda s, f, *_: (s, 0, 0), memory_space=pltpu.SMEM),
            pl.BlockSpec(memory_space=pl.ANY),
            pl.BlockSpec((None, D_MODEL, MOE_TF), lambda s, f, e, t0, nt: (e[s], 0, ff(s, f, e, t0, nt))),
            pl.BlockSpec((None, D_MODEL, MOE_TF), lambda s, f, e, t0, nt: (e[s], 0, MOE_NF + ff(s, f, e, t0, nt))),
            pl.BlockSpec((None, 1, MOE_TF), lambda s, f, e, t0, nt: (e[s], 0, ff(s, f, e, t0, nt))),
            pl.BlockSpec((None, 1, MOE_TF), lambda s, f, e, t0, nt: (e[s], 0, MOE_NF + ff(s, f, e, t0, nt))),
            pl.BlockSpec((None, MOE_TF, D_MODEL), lambda s, f, e, t0, nt: (e[s], ff(s, f, e, t0, nt), 0)),
            pl.BlockSpec((None, 1, D_MODEL), lambda s, f, e, t0, nt: (e[s], 0, 0)),
        ],
        out_specs=pl.BlockSpec(memory_space=pl.ANY),
        scratch_shapes=[
            pltpu.VMEM((MOE_SB_ROWS, D_MODEL), F32),
            pltpu.VMEM((MOE_SB_ROWS, D_MODEL), BF16),
            pltpu.VMEM((MOE_SB_ROWS, D_MODEL), F32),
            pltpu.VMEM((D_MODEL, 2 * MOE_TF), BF16),
            pltpu.VMEM((MOE_TF, D_MODEL), BF16),
            pltpu.SemaphoreType.DMA(()),
            pltpu.SemaphoreType.DMA(()),
        ],
    )
    return pl.pallas_call(
        _moe_expert_kernel,
        grid_spec=grid_spec,
        out_shape=jax.ShapeDtypeStruct((n_slots, D_MODEL), F32),
        compiler_params=pltpu.CompilerParams(
            dimension_semantics=("arbitrary", "arbitrary"),
            vmem_limit_bytes=VMEM_LIMIT_BYTES),
        name="moe_experts",
    )(sb_e, sb_tile0, sb_ntile, sb_tok, x,
      w_up, w_up, b_up.reshape(N_EXPERTS, 1, 2 * D_FF), b_up.reshape(N_EXPERTS, 1, 2 * D_FF),
      w_down, b_down.reshape(N_EXPERTS, 1, D_MODEL))


def _moe_ffn(x, w_router, b_router, w_up, b_up, w_down, b_down):
    n, d = x.shape
    logits = (x @ w_router + b_router).astype(F32)
    top_vals, top_idx = lax.top_k(logits, TOP_K)
    gates = jax.nn.softmax(top_vals, axis=-1).astype(x.dtype)
    nk = n * TOP_K
    e_flat = top_idx.reshape(nk)
    order = jnp.argsort(e_flat)
    e_sorted = e_flat[order]
    tok_sorted = (order // TOP_K).astype(jnp.int32)
    gate_sorted = gates.reshape(nk)[order]
    counts = jnp.zeros((N_EXPERTS,), jnp.int32).at[e_flat].add(1)
    tile_start, sb_e, sb_tile0, sb_ntile = _moe_plan(counts, nk)
    start = jnp.cumsum(counts) - counts
    slot = tile_start[e_sorted] * MOE_TM + jnp.arange(nk, dtype=jnp.int32) - start[e_sorted]
    n_slots = _moe_n_tiles(nk) * MOE_TM
    slot_tok = jnp.zeros((n_slots,), jnp.int32).at[slot].set(tok_sorted)
    slot_gate = jnp.zeros((n_slots,), x.dtype).at[slot].set(gate_sorted)
    ys = _moe_experts(x, slot_tok, sb_e, sb_tile0, sb_ntile, w_up, b_up, w_down, b_down)
    ys = jnp.where((slot_gate != 0)[:, None], ys * slot_gate[:, None], 0.0)
    return jnp.zeros_like(x).at[slot_tok].add(ys)


def _rms_norm(x, gain):
    xf = x.astype(F32)
    y = xf * lax.rsqrt(jnp.mean(xf * xf, axis=-1, keepdims=True) + NORM_EPS)
    return y.astype(x.dtype) * gain


def _modulate(h, shift, scale):
    return h * (1 + scale[:, None]) + shift[:, None]


def _t5_bucket(dist):
    dist = jnp.maximum(dist, 0)
    max_exact = N_BUCKETS // 2
    d = dist.astype(F32)
    ratio = jnp.log(jnp.maximum(d, 1.0) / max_exact) / math.log(MAX_DISTANCE / max_exact)
    large = max_exact + (ratio * (N_BUCKETS - max_exact)).astype(jnp.int32)
    large = jnp.minimum(large, N_BUCKETS - 1)
    return jnp.where(dist < max_exact, dist, large)


def _masked_softmax(s, valid, sink):
    s = jnp.where(valid, s, NEG_INF)
    m = jnp.max(s, axis=-1, keepdims=True)
    if sink is not None:
        m = jnp.maximum(m, sink)
    p = jnp.exp(s - m)
    denom = jnp.sum(p, axis=-1, keepdims=True)
    if sink is not None:
        denom = denom + jnp.exp(sink - m)
    return p / denom, (m + jnp.log(denom))[..., 0]


def _banded_window_attention(q, k, v, window, dilation, rel_bias, sinks):
    n, L, h, hd = q.shape
    kvh = k.shape[2]
    g = h // kvh
    blk = min(ATTN_BLOCK, L)
    nb = -(-L // blk)
    qb = q.reshape(n, nb, blk, kvh, g, hd)
    kb = k.reshape(n, nb, blk, kvh, hd)
    vb = v.reshape(n, nb, blk, kvh, hd)
    prev = ((0, 0), (1, 0), (0, 0), (0, 0), (0, 0))
    kw = jnp.concatenate([jnp.pad(kb[:, :-1], prev), kb], axis=2)
    vw = jnp.concatenate([jnp.pad(vb[:, :-1], prev), vb], axis=2)
    s = jnp.einsum('nbqhgd,nbkhd->nbhgqk', qb, kw).astype(F32) * hd ** -0.5
    qi = jnp.arange(blk)[:, None]
    kj = jnp.arange(2 * blk)[None, :]
    dist = blk + qi - kj
    bias = rel_bias[_t5_bucket(dist * dilation)].astype(F32)
    bias = jnp.transpose(bias, (2, 0, 1)).reshape(kvh, g, blk, 2 * blk)
    in_window = (dist >= 0) & (dist <= window)
    has_prev = (jnp.arange(nb) > 0)[:, None, None] | (kj >= blk)[None]
    valid = (in_window[None] & has_prev)[None, :, None, None]
    sink = None if sinks is None else sinks.astype(F32).reshape(kvh, g, 1, 1)
    p, lse = _masked_softmax(s + bias, valid, sink)
    o = jnp.einsum('nbhgqk,nbkhd->nbqhgd', p.astype(v.dtype), vw)
    o = o.reshape(n, nb * blk, h, hd)[:, :L]
    lse = jnp.transpose(lse, (0, 1, 4, 2, 3)).reshape(n, nb * blk, h)[:, :L]
    return o, lse


def _window_decode_attention(q, k_all, v_all, window, dilation, rel_bias, sinks):
    bd, t, h, hd = q.shape
    kvh = k_all.shape[2]
    g = h // kvh
    n_buf = k_all.shape[1] - t
    steps = jnp.arange(window // dilation + 1)
    idx = n_buf + jnp.arange(t)[:, None] - dilation * steps[None, :]
    valid = idx >= 0
    idx = jnp.maximum(idx, 0)
    kg = jnp.take(k_all, idx, axis=1)
    vg = jnp.take(v_all, idx, axis=1)
    qg = q.reshape(bd, t, kvh, g, hd)
    s = jnp.einsum('bthgd,btjhd->bhgtj', qg, kg).astype(F32) * hd ** -0.5
    bias = rel_bias[_t5_bucket(dilation * steps)].astype(F32)
    bias = bias.T.reshape(kvh, g, 1, steps.shape[0])
    sink = None if sinks is None else sinks.astype(F32).reshape(kvh, g, 1, 1)
    p, lse = _masked_softmax(s + bias, valid[None, None, None], sink)
    o = jnp.einsum('bhgtj,btjhd->bthgd', p.astype(v_all.dtype), vg).reshape(bd, t, h, hd)
    lse = jnp.transpose(lse, (0, 3, 1, 2)).reshape(bd, t, h)
    return o, lse


def _to_residues(x, d):
    b, s = x.shape[:2]
    rest = x.shape[2:]
    return jnp.swapaxes(x.reshape(b, s // d, d, *rest), 1, 2).reshape(b * d, s // d, *rest)


def _from_residues(x, b, d):
    l = x.shape[1]
    rest = x.shape[2:]
    return jnp.swapaxes(x.reshape(b, d, l, *rest), 1, 2).reshape(b, l * d, *rest)


def _combine_groups(outs, lses):
    w = jax.nn.softmax(jnp.stack(lses, axis=0), axis=0)
    return jnp.sum(w[..., None] * jnp.stack(outs, axis=0), axis=0)


def _split_qkv_a(h, w_qkv, qk_gain):
    b, s, _ = h.shape
    qkv = h @ w_qkv
    nq = N_HEADS * HEAD_DIM
    nk = KV_HEADS * HEAD_DIM
    q = qkv[..., :nq].reshape(b, s, N_HEADS, HEAD_DIM)
    k = qkv[..., nq:nq + nk].reshape(b, s, KV_HEADS, HEAD_DIM)
    v = qkv[..., nq + nk:].reshape(b, s, KV_HEADS, HEAD_DIM)
    return _rms_norm(q, qk_gain[0]), _rms_norm(k, qk_gain[1]), v


def _split_qkv_b(h, w_qkv, qk_gain):
    b, s, _ = h.shape
    qkv = h @ w_qkv
    nq = N_GROUPS_B * N_HEADS * HEAD_DIM
    nk = N_GROUPS_B * KV_HEADS * HEAD_DIM
    q = qkv[..., :nq].reshape(b, s, N_GROUPS_B, N_HEADS, HEAD_DIM)
    k = qkv[..., nq:nq + nk].reshape(b, s, N_GROUPS_B, KV_HEADS, HEAD_DIM)
    v = qkv[..., nq + nk:].reshape(b, s, N_GROUPS_B, KV_HEADS, HEAD_DIM)
    return _rms_norm(q, qk_gain[0]), _rms_norm(k, qk_gain[1]), v


def _mixer_a_prompt(h, w_qkv, qk_gain, sinks, w_o, rel_bias):
    b, s, _ = h.shape
    q, k, v = _split_qkv_a(h, w_qkv, qk_gain)
    o, _ = _banded_window_attention(q, k, v, WINDOW_A, 1, rel_bias, sinks)
    w = min(WINDOW_A, s)
    state = jnp.stack([k[:, s - w:], v[:, s - w:]], axis=1)
    return o.reshape(b, s, ATTN_WIDTH) @ w_o, state


def _mixer_a_sample(h, cache, w_qkv, qk_gain, sinks, w_o, rel_bias):
    bd, t, _ = h.shape
    q, k, v = _split_qkv_a(h, w_qkv, qk_gain)
    k_all = jnp.concatenate([cache[:, 0].astype(k.dtype), k], axis=1)
    v_all = jnp.concatenate([cache[:, 1].astype(v.dtype), v], axis=1)
    o, _ = _window_decode_attention(q, k_all, v_all, WINDOW_A, 1, rel_bias, sinks)
    w = cache.shape[2]
    state = jnp.stack([k_all[:, -w:], v_all[:, -w:]], axis=1)
    return o.reshape(bd, t, ATTN_WIDTH) @ w_o, state


def _mixer_b_prompt(h, w_qkv, qk_gain, w_o, rel_bias):
    b, s, _ = h.shape
    q, k, v = _split_qkv_b(h, w_qkv, qk_gain)
    outs, lses, states = [], [], []
    for gi, (win, dil) in enumerate(B_PATTERNS):
        o, lse = _banded_window_attention(_to_residues(q[:, :, gi], dil), _to_residues(k[:, :, gi], dil),
                                          _to_residues(v[:, :, gi], dil), win // dil, dil, rel_bias, None)
        outs.append(_from_residues(o, b, dil))
        lses.append(_from_residues(lse, b, dil))
        w = min(win, s)
        states.append(jnp.stack([k[:, s - w:, gi], v[:, s - w:, gi]], axis=1))
    o = _combine_groups(outs, lses).astype(h.dtype)
    return o.reshape(b, s, ATTN_WIDTH) @ w_o, states


def _mixer_b_sample(h, caches, w_qkv, qk_gain, w_o, rel_bias):
    bd, t, _ = h.shape
    q, k, v = _split_qkv_b(h, w_qkv, qk_gain)
    outs, lses, states = [], [], []
    for gi, (win, dil) in enumerate(B_PATTERNS):
        cache = caches[gi]
        k_all = jnp.concatenate([cache[:, 0].astype(k.dtype), k[:, :, gi]], axis=1)
        v_all = jnp.concatenate([cache[:, 1].astype(v.dtype), v[:, :, gi]], axis=1)
        o, lse = _window_decode_attention(q[:, :, gi], k_all, v_all, win, dil, rel_bias, None)
        outs.append(o)
        lses.append(lse)
        w = cache.shape[2]
        states.append(jnp.stack([k_all[:, -w:], v_all[:, -w:]], axis=1))
    o = _combine_groups(outs, lses).astype(h.dtype)
    return o.reshape(bd, t, ATTN_WIDTH) @ w_o, states


def kernel(x_prompt, x_sample, c_prompt, c_sample, cache_a_kv, cache_b1_kv, cache_b2_kv, cache_b3_kv,
           rel_bias, norm_gain, w_ada, b_ada, w_qkv_a, qk_gain_a, sinks_a, w_o_a,
           w_qkv_b, qk_gain_b, w_o_b, w_router, b_router, w_up, b_up, w_down, b_down):
    b, s, d = x_prompt.shape
    bd, t, _ = x_sample.shape
    xp, xs = x_prompt, x_sample
    a_p, a_s = [], []
    b_p = [[], [], []]
    b_s = [[], [], []]
    caches_b = (cache_b1_kv, cache_b2_kv, cache_b3_kv)
    for i in range(DEPTH):
        mp = jnp.split(jax.nn.silu(c_prompt) @ w_ada[i] + b_ada[i], 6, axis=-1)
        ms = jnp.split(jax.nn.silu(c_sample) @ w_ada[i] + b_ada[i], 6, axis=-1)
        hp = _modulate(_rms_norm(xp, norm_gain[i, 0]), mp[0], mp[1])
        hs = _modulate(_rms_norm(xs, norm_gain[i, 0]), ms[0], ms[1])
        j = i // 2
        if i % 2 == 0:
            yp, st_p = _mixer_a_prompt(hp, w_qkv_a[j], qk_gain_a[j], sinks_a[j], w_o_a[j], rel_bias)
            ys, st_s = _mixer_a_sample(hs, cache_a_kv[j], w_qkv_a[j], qk_gain_a[j], sinks_a[j], w_o_a[j], rel_bias)
            a_p.append(st_p)
            a_s.append(st_s)
        else:
            yp, st_p = _mixer_b_prompt(hp, w_qkv_b[j], qk_gain_b[j], w_o_b[j], rel_bias)
            ys, st_s = _mixer_b_sample(hs, [c[j] for c in caches_b], w_qkv_b[j], qk_gain_b[j], w_o_b[j], rel_bias)
            for gi in range(N_GROUPS_B):
                b_p[gi].append(st_p[gi])
                b_s[gi].append(st_s[gi])
        xp = xp + mp[2][:, None] * yp
        xs = xs + ms[2][:, None] * ys
        hp = _modulate(_rms_norm(xp, norm_gain[i, 1]), mp[3], mp[4])
        hs = _modulate(_rms_norm(xs, norm_gain[i, 1]), ms[3], ms[4])
        f = _moe_ffn(jnp.concatenate([hp.reshape(b * s, d), hs.reshape(bd * t, d)], axis=0),
                     w_router[i], b_router[i], w_up[i], b_up[i], w_down[i], b_down[i])
        xp = xp + mp[5][:, None] * f[:b * s].reshape(b, s, d)
        xs = xs + ms[5][:, None] * f[b * s:].reshape(bd, t, d)
    return (xp, xs,
            jnp.stack(a_p), jnp.stack(a_s),
            jnp.stack(b_p[0]), jnp.stack(b_s[0]),
            jnp.stack(b_p[1]), jnp.stack(b_s[1]),
            jnp.stack(b_p[2]), jnp.stack(b_s[2]))
```

```python
import functools
import math

import jax
import jax.numpy as jnp
from jax import lax
from jax.experimental import pallas as pl
from jax.experimental.pallas import tpu as pltpu

D_MODEL = 2048
DEPTH = 4
HEAD_DIM = 64
N_HEADS = 32
KV_HEADS = 4
HEADS_PER_KV = N_HEADS // KV_HEADS
KV_WIDTH = KV_HEADS * HEAD_DIM
WINDOW_A = 128
B_PATTERNS = ((128, 1), (512, 4), (2048, 16))
N_GROUPS_B = 3
N_BUCKETS = 32
MAX_DISTANCE = 2048
N_EXPERTS = 32
TOP_K = 4
D_FF = 2048
SWIGLU_ALPHA = 1.702
SWIGLU_LIMIT = 7.0
ATTN_BLOCK = 128
NORM_EPS = 1e-6
NEG_INF = -1e30
ATTN_WIDTH = N_HEADS * HEAD_DIM
QKV_A = ATTN_WIDTH + 2 * KV_WIDTH
QKV_B = N_GROUPS_B * QKV_A

F32 = jnp.float32
BF16 = jnp.bfloat16
I32 = jnp.int32

VMEM_LIMIT_BYTES = 56 * 1024 * 1024
MXU_DIM = 256

N_MOD_ROWS = 40
MOD_PROMPT_ROW0 = 32
ROW_TILE = 256
MM_BM = 512
ADA_BN = 1024

MOE_TM = 256
MOE_SB_TILES = 5
MOE_SB_ROWS = MOE_TM * MOE_SB_TILES
MOE_TF = 256
MOE_NF = D_FF // MOE_TF


def _cparams(*sem):
    return pltpu.CompilerParams(dimension_semantics=sem, vmem_limit_bytes=VMEM_LIMIT_BYTES)


def _dot(a, b):
    return jnp.dot(a, b, preferred_element_type=F32)


def _dot_nt(a, b):
    return lax.dot_general(a, b, (((1,), (1,)), ((), ())), preferred_element_type=F32)


def _ada_kernel(c_ref, w_ref, b_ref, o_ref):
    c = c_ref[...]
    a = (c / (1.0 + jnp.exp(-c))).astype(BF16)
    o_ref[...] = _dot(a, w_ref[...].astype(BF16)) + b_ref[...]


def _ada(c_all, w_ada, b_ada):
    n6 = w_ada.shape[-1]
    return pl.pallas_call(
        _ada_kernel,
        grid=(DEPTH, n6 // ADA_BN),
        in_specs=[pl.BlockSpec((N_MOD_ROWS, D_MODEL), lambda i, n: (0, 0)),
                  pl.BlockSpec((None, D_MODEL, ADA_BN), lambda i, n: (i, 0, n)),
                  pl.BlockSpec((None, 1, ADA_BN), lambda i, n: (i, 0, n))],
        out_specs=pl.BlockSpec((None, N_MOD_ROWS, ADA_BN), lambda i, n: (i, 0, n)),
        out_shape=jax.ShapeDtypeStruct((DEPTH, N_MOD_ROWS, n6), F32),
        compiler_params=_cparams("arbitrary", "arbitrary"),
        name="ada",
    )(c_all, w_ada, b_ada.reshape(DEPTH, 1, n6))


class _Mods:
    def __init__(self, mods, layer):
        self.m3 = mods
        self.m4 = mods.reshape(DEPTH, N_MOD_ROWS, 1, mods.shape[-1])
        self.layer = layer

    def prompt_spec(self, chunk, width, seq_of_step, col_of_step=None):
        i, per = self.layer, D_MODEL // width

        def imap(*g):
            col = 0 if col_of_step is None else col_of_step(*g)
            return (i, MOD_PROMPT_ROW0 + seq_of_step(*g), 0, chunk * per + col)
        return pl.BlockSpec((None, None, 1, width), imap)

    def sample_spec(self, chunk, width, n_rows, col_of_step=None):
        i, per = self.layer, D_MODEL // width

        def imap(*g):
            col = 0 if col_of_step is None else col_of_step(*g)
            return (i, 0, chunk * per + col)
        return pl.BlockSpec((None, n_rows, width), imap)


def _normmod(x, gain, shift, scale):
    y = x * lax.rsqrt(jnp.mean(x * x, axis=-1, keepdims=True) + NORM_EPS)
    return (y * gain) * (1.0 + scale) + shift


def _normmod_kernel(x_ref, g_ref, sh_ref, sc_ref, o_ref):
    o_ref[...] = _normmod(x_ref[...], g_ref[...], sh_ref[...], sc_ref[...]).astype(o_ref.dtype)


def _normmod_rows(xp, xs, gain, mods, c_shift, c_scale):
    n_p, n_s = xp.shape[0], xs.shape[0]
    seq_tiles = _seq_rows(xp) // ROW_TILE
    gspec = pl.BlockSpec((1, D_MODEL), lambda t: (0, 0))
    hp = pl.pallas_call(
        _normmod_kernel,
        grid=(n_p // ROW_TILE,),
        in_specs=[pl.BlockSpec((ROW_TILE, D_MODEL), lambda t: (t, 0)), gspec,
                  mods.prompt_spec(c_shift, D_MODEL, lambda t: t // seq_tiles),
                  mods.prompt_spec(c_scale, D_MODEL, lambda t: t // seq_tiles)],
        out_specs=pl.BlockSpec((ROW_TILE, D_MODEL), lambda t: (t, 0)),
        out_shape=jax.ShapeDtypeStruct((n_p, D_MODEL), BF16),
        compiler_params=_cparams("arbitrary"),
        name="normmod_prompt",
    )(xp, gain, mods.m4, mods.m4)
    hs = pl.pallas_call(
        _normmod_kernel,
        grid=(1,),
        in_specs=[pl.BlockSpec((n_s, D_MODEL), lambda t: (0, 0)), gspec,
                  mods.sample_spec(c_shift, D_MODEL, n_s), mods.sample_spec(c_scale, D_MODEL, n_s)],
        out_specs=pl.BlockSpec((n_s, D_MODEL), lambda t: (0, 0)),
        out_shape=jax.ShapeDtypeStruct((n_s, D_MODEL), BF16),
        compiler_params=_cparams("arbitrary"),
        name="normmod_sample",
    )(xs, gain, mods.m3, mods.m3)
    return hp, hs


_SEQ_ROWS = 2048


def _seq_rows(xp):
    return _SEQ_ROWS


def _head_sumsq(y):
    y2 = y * y
    hi = y2.astype(BF16)
    lo = (y2 - hi.astype(F32)).astype(BF16)
    r = lax.broadcasted_iota(I32, (MXU_DIM, MXU_DIM), 0) // HEAD_DIM
    c = lax.broadcasted_iota(I32, (MXU_DIM, MXU_DIM), 1) // HEAD_DIM
    seg = jnp.where(r == c, 1.0, 0.0).astype(BF16)
    outs = []
    for j in range(y.shape[1] // MXU_DIM):
        sl = slice(j * MXU_DIM, (j + 1) * MXU_DIM)
        outs.append(_dot(hi[:, sl], seg) + _dot(lo[:, sl], seg))
    return jnp.concatenate(outs, axis=1)


def _qkv_epilogue(acc, gain, flag):
    inv = lax.rsqrt(_head_sumsq(acc) * (1.0 / HEAD_DIM) + NORM_EPS)
    return jnp.where(flag > 0.5, (acc * inv) * gain, acc)


def _mm_qkv_kernel(n_ptiles, ap_ref, as_ref, w_ref, gain_ref, flag_ref, op_ref, os_ref, wb_ref):
    t = pl.program_id(1)

    @pl.when(t == 0)
    def _():
        wb_ref[...] = w_ref[...].astype(BF16)

    @pl.when(t < n_ptiles)
    def _():
        op_ref[...] = _qkv_epilogue(_dot(ap_ref[...], wb_ref[...]), gain_ref[...], flag_ref[...])

    @pl.when(t == n_ptiles)
    def _():
        os_ref[...] = _qkv_epilogue(_dot(as_ref[...], wb_ref[...]), gain_ref[...], flag_ref[...])


def _mm_qkv(hp, hs, w, col_gain, col_flag, bn):
    n_p, n_s, n = hp.shape[0], hs.shape[0], w.shape[1]
    n_pt = n_p // MM_BM
    pt = lambda c, t: (jnp.minimum(t, n_pt - 1), 0)
    return pl.pallas_call(
        functools.partial(_mm_qkv_kernel, n_pt),
        grid=(n // bn, n_pt + 1),
        in_specs=[pl.BlockSpec((MM_BM, D_MODEL), pt),
                  pl.BlockSpec((n_s, D_MODEL), lambda c, t: (0, 0)),
                  pl.BlockSpec((D_MODEL, bn), lambda c, t: (0, c)),
                  pl.BlockSpec((1, bn), lambda c, t: (0, c)),
                  pl.BlockSpec((1, bn), lambda c, t: (0, c))],
        out_specs=[pl.BlockSpec((MM_BM, bn), lambda c, t: (jnp.minimum(t, n_pt - 1), c)),
                   pl.BlockSpec((n_s, bn), lambda c, t: (0, c))],
        out_shape=[jax.ShapeDtypeStruct((n_p, n), F32), jax.ShapeDtypeStruct((n_s, n), F32)],
        scratch_shapes=[pltpu.VMEM((D_MODEL, bn), BF16)],
        compiler_params=_cparams("arbitrary", "arbitrary"),
        name="mm_qkv",
    )(hp, hs, w, col_gain, col_flag)


def _mm_resid_kernel(n_ptiles, ap_ref, as_ref, w_ref, xp_ref, xs_ref, gp_ref, gs_ref, op_ref, os_ref, wb_ref):
    t = pl.program_id(1)

    @pl.when(t == 0)
    def _():
        wb_ref[...] = w_ref[...].astype(BF16)

    @pl.when(t < n_ptiles)
    def _():
        op_ref[...] = xp_ref[...] + gp_ref[...] * _dot(ap_ref[...], wb_ref[...])

    @pl.when(t == n_ptiles)
    def _():
        os_ref[...] = xs_ref[...] + gs_ref[...] * _dot(as_ref[...], wb_ref[...])


def _mm_resid(ap, as_, w, xp, xs, mods, c_gate, bn):
    n_p, n_s, k, n = ap.shape[0], as_.shape[0], w.shape[0], w.shape[1]
    n_pt = n_p // MM_BM
    seq_tiles = _seq_rows(xp) // MM_BM
    prow = lambda c, t: jnp.minimum(t, n_pt - 1)
    return pl.pallas_call(
        functools.partial(_mm_resid_kernel, n_pt),
        grid=(n // bn, n_pt + 1),
        in_specs=[pl.BlockSpec((MM_BM, k), lambda c, t: (prow(c, t), 0)),
                  pl.BlockSpec((n_s, k), lambda c, t: (0, 0)),
                  pl.BlockSpec((k, bn), lambda c, t: (0, c)),
                  pl.BlockSpec((MM_BM, bn), lambda c, t: (prow(c, t), c)),
                  pl.BlockSpec((n_s, bn), lambda c, t: (0, c)),
                  mods.prompt_spec(c_gate, bn, lambda c, t: prow(c, t) // seq_tiles, lambda c, t: c),
                  mods.sample_spec(c_gate, bn, n_s, lambda c, t: c)],
        out_specs=[pl.BlockSpec((MM_BM, bn), lambda c, t: (prow(c, t), c)),
                   pl.BlockSpec((n_s, bn), lambda c, t: (0, c))],
        out_shape=[jax.ShapeDtypeStruct((n_p, n), F32), jax.ShapeDtypeStruct((n_s, n), F32)],
        scratch_shapes=[pltpu.VMEM((k, bn), BF16)],
        compiler_params=_cparams("arbitrary", "arbitrary"),
        name="mm_resid",
    )(ap, as_, w, xp, xs, mods.m4, mods.m3)


def _attn_kernel(has_sink, with_lse, *refs):
    if has_sink:
        sink_ref, refs = refs[0], refs[1:]
    q_refs = refs[:KV_HEADS]
    kc_ref, kp_ref, vc_ref, vp_ref, bias_ref = refs[KV_HEADS:KV_HEADS + 5]
    o_ref = refs[KV_HEADS + 5]
    lse_ref = refs[KV_HEADS + 6] if with_lse else None

    kw = jnp.concatenate([kp_ref[...], kc_ref[...]], axis=0).astype(BF16)
    vw = jnp.concatenate([vp_ref[...], vc_ref[...]], axis=0).astype(BF16)
    lane = lax.broadcasted_iota(I32, (ATTN_BLOCK, 128), 1)
    lse_all = jnp.zeros((ATTN_BLOCK, 128), F32)
    for kh in range(KV_HEADS):
        k_h = kw[:, kh * HEAD_DIM:(kh + 1) * HEAD_DIM]
        q_all = q_refs[kh][...].astype(BF16)
        outs = []
        for g in range(HEADS_PER_KV):
            h = kh * HEADS_PER_KV + g
            s = _dot_nt(q_all[:, g * HEAD_DIM:(g + 1) * HEAD_DIM], k_h) + bias_ref[h]
            m = jnp.max(s, axis=-1, keepdims=True)
            if has_sink:
                m = jnp.maximum(m, sink_ref[h])
            p = jnp.exp(s - m)
            den = jnp.sum(p, axis=-1, keepdims=True)
            if has_sink:
                den = den + jnp.exp(sink_ref[h] - m)
            o_full = _dot(p.astype(BF16), vw)
            outs.append(o_full[:, kh * HEAD_DIM:(kh + 1) * HEAD_DIM] / den)
            if with_lse:
                lse_all = jnp.where(lane == h, m + jnp.log(den), lse_all)
        lo = kh * HEADS_PER_KV * HEAD_DIM
        o_ref[:, lo:lo + HEADS_PER_KV * HEAD_DIM] = jnp.concatenate(outs, axis=1).astype(o_ref.dtype)
    if with_lse:
        lse_ref[...] = lse_all


def _prompt_attention(qkv_p, width, q_off, k_off, v_off, dil, bias, sinks, out_dtype, with_lse):
    n_p = qkv_p.shape[0]
    rows = n_p // dil
    view = qkv_p.reshape(rows, dil * width)
    nb_seq = _seq_rows(qkv_p) // dil // ATTN_BLOCK
    qw = HEADS_PER_KV * HEAD_DIM
    has_sink = sinks is not None

    def qspec(kh):
        return pl.BlockSpec((ATTN_BLOCK, qw), lambda rb, r: (rb, r * (width // qw) + q_off // qw + kh))

    def kvspec(off, prev):
        def imap(rb, r):
            return (jnp.maximum(rb - 1, 0) if prev else rb, r * (width // KV_WIDTH) + off // KV_WIDTH)
        return pl.BlockSpec((ATTN_BLOCK, KV_WIDTH), imap)

    in_specs = [qspec(kh) for kh in range(KV_HEADS)]
    in_specs += [kvspec(k_off, False), kvspec(k_off, True), kvspec(v_off, False), kvspec(v_off, True),
                 pl.BlockSpec((None, N_HEADS, ATTN_BLOCK, 2 * ATTN_BLOCK),
                              lambda rb, r: (jnp.where(rb % nb_seq == 0, 1, 0), 0, 0, 0))]
    args = [view] * (KV_HEADS + 4) + [bias]
    if has_sink:
        in_specs = [pl.BlockSpec(memory_space=pltpu.SMEM)] + in_specs
        args = [sinks] + args
    out_specs = [pl.BlockSpec((ATTN_BLOCK, ATTN_WIDTH), lambda rb, r: (rb, r))]
    out_shape = [jax.ShapeDtypeStruct((rows, dil * ATTN_WIDTH), out_dtype)]
    if with_lse:
        out_specs.append(pl.BlockSpec((ATTN_BLOCK, 128), lambda rb, r: (rb, r)))
        out_shape.append(jax.ShapeDtypeStruct((rows, dil * 128), F32))
    res = pl.pallas_call(
        functools.partial(_attn_kernel, has_sink, with_lse),
        grid=(rows // ATTN_BLOCK, dil),
        in_specs=in_specs, out_specs=out_specs, out_shape=out_shape,
        compiler_params=_cparams("arbitrary", "arbitrary"),
        name="prompt_attention",
    )(*args)
    o = res[0].reshape(n_p, ATTN_WIDTH)
    if with_lse:
        return o, res[1].reshape(n_p, 128)
    return o


def _t5_bucket(dist):
    dist = jnp.maximum(dist, 0)
    max_exact = N_BUCKETS // 2
    d = dist.astype(F32)
    ratio = jnp.log(jnp.maximum(d, 1.0) / max_exact) / math.log(MAX_DISTANCE / max_exact)
    large = max_exact + (ratio * (N_BUCKETS - max_exact)).astype(I32)
    large = jnp.minimum(large, N_BUCKETS - 1)
    return jnp.where(dist < max_exact, dist, large)


def _prompt_bias(rel_bias, dil):
    qi = jnp.arange(ATTN_BLOCK)[:, None]
    kj = jnp.arange(2 * ATTN_BLOCK)[None, :]
    dist = ATTN_BLOCK + qi - kj
    bias = jnp.transpose(rel_bias[_t5_bucket(dist * dil)].astype(F32), (2, 0, 1))
    valid = (dist >= 0) & (dist <= ATTN_BLOCK)
    first = valid & (kj >= ATTN_BLOCK)
    return jnp.stack([jnp.where(valid[None], bias, NEG_INF), jnp.where(first[None], bias, NEG_INF)])


def _decode_bias(rel_bias, dil):
    steps = jnp.arange(ATTN_BLOCK + 1)
    b = rel_bias[_t5_bucket(dil * steps)].astype(F32)
    cache = b[1:][::-1].T
    new = b[0][:, None]
    return jnp.concatenate([cache, new, jnp.zeros((N_HEADS, ATTN_BLOCK - 1), F32)], axis=1)


def _bcast_heads(w):
    r = lax.broadcasted_iota(I32, (128, ATTN_WIDTH), 0)
    c = lax.broadcasted_iota(I32, (128, ATTN_WIDTH), 1) // HEAD_DIM
    e = jnp.where(r == c, 1.0, 0.0).astype(BF16)
    hi = w.astype(BF16)
    r1 = w - hi.astype(F32)
    mid = r1.astype(BF16)
    lo = (r1 - mid.astype(F32)).astype(BF16)
    return _dot(hi, e) + _dot(mid, e) + _dot(lo, e)


def _combine_kernel(o1_ref, o2_ref, o3_ref, l1_ref, l2_ref, l3_ref, o_ref):
    l1, l2, l3 = l1_ref[...], l2_ref[...], l3_ref[...]
    m = jnp.maximum(jnp.maximum(l1, l2), l3)
    e1, e2, e3 = jnp.exp(l1 - m), jnp.exp(l2 - m), jnp.exp(l3 - m)
    inv = 1.0 / (e1 + e2 + e3)
    acc = _bcast_heads(e1 * inv) * o1_ref[...]
    acc = acc + _bcast_heads(e2 * inv) * o2_ref[...]
    acc = acc + _bcast_heads(e3 * inv) * o3_ref[...]
    o_ref[...] = acc.astype(o_ref.dtype)


def _combine_groups(outs, lses):
    n_p = outs[0].shape[0]
    ospec = pl.BlockSpec((ROW_TILE, ATTN_WIDTH), lambda t: (t, 0))
    lspec = pl.BlockSpec((ROW_TILE, 128), lambda t: (t, 0))
    return pl.pallas_call(
        _combine_kernel,
        grid=(n_p // ROW_TILE,),
        in_specs=[ospec] * 3 + [lspec] * 3,
        out_specs=ospec,
        out_shape=jax.ShapeDtypeStruct((n_p, ATTN_WIDTH), BF16),
        compiler_params=_cparams("arbitrary"),
        name="combine_groups",
    )(*outs, *lses)


def _decode_kernel(groups, has_sink, *refs):
    ng = len(groups)
    q_ref, kv_ref, bias_ref = refs[:3]
    pos = 3
    if has_sink:
        sink_ref = refs[pos]
        pos += 1
    cache_refs = refs[pos:pos + ng]
    state_refs = refs[pos + ng:pos + 2 * ng]
    o_ref = refs[pos + 2 * ng]

    row_kh = lax.broadcasted_iota(I32, (N_HEADS, HEAD_DIM), 0) // HEADS_PER_KV
    outs, lses = [], []
    for g, (win, dil) in enumerate(groups):
        c_ref, st_ref = cache_refs[g], state_refs[g]
        k_new = kv_ref[:, g * KV_WIDTH:(g + 1) * KV_WIDTH]
        v_new = kv_ref[:, (ng + g) * KV_WIDTH:(ng + g + 1) * KV_WIDTH]
        for kv, new in ((0, k_new), (1, v_new)):
            st_ref[kv, pl.ds(0, 2 * win - 2), :] = c_ref[kv, pl.ds(2, 2 * win - 2), :]
            st_ref[kv, pl.ds(2 * win - 2, 1), :] = new[:, :128]
            st_ref[kv, pl.ds(2 * win - 1, 1), :] = new[:, 128:]
        k_sel = jnp.concatenate([c_ref[0, pl.ds(half, ATTN_BLOCK, stride=2 * dil), :] for half in (0, 1)], axis=1)
        v_sel = jnp.concatenate([c_ref[1, pl.ds(half, ATTN_BLOCK, stride=2 * dil), :] for half in (0, 1)], axis=1)
        q = q_ref[g]
        q_exp = jnp.concatenate([jnp.where(row_kh == kh, q, 0.0) for kh in range(KV_HEADS)], axis=1)
        qb = q_exp.astype(BF16)
        bias = bias_ref[g]
        s = _dot_nt(qb, k_sel.astype(BF16)) + bias[:, :ATTN_BLOCK]
        s0 = jnp.sum(qb.astype(F32) * k_new.astype(BF16).astype(F32), axis=-1, keepdims=True)
        s0 = s0 + bias[:, ATTN_BLOCK:ATTN_BLOCK + 1]
        m = jnp.maximum(jnp.max(s, axis=-1, keepdims=True), s0)
        if has_sink:
            m = jnp.maximum(m, sink_ref[...])
        p = jnp.exp(s - m)
        p0 = jnp.exp(s0 - m)
        den = jnp.sum(p, axis=-1, keepdims=True) + p0
        if has_sink:
            den = den + jnp.exp(sink_ref[...] - m)
        o_full = _dot(p.astype(BF16), v_sel.astype(BF16))
        o_full = o_full + p0.astype(BF16).astype(F32) * v_new.astype(BF16).astype(F32)
        o_h = jnp.zeros((N_HEADS, HEAD_DIM), F32)
        for kh in range(KV_HEADS):
            o_h = o_h + jnp.where(row_kh == kh, o_full[:, kh * HEAD_DIM:(kh + 1) * HEAD_DIM], 0.0)
        outs.append(o_h / den)
        lses.append(m + jnp.log(den))
    if ng == 1:
        o = outs[0]
    else:
        m = functools.reduce(jnp.maximum, lses)
        es = [jnp.exp(l - m) for l in lses]
        tot = functools.reduce(lambda a, b: a + b, es)
        o = functools.reduce(lambda a, b: a + b, [(e / tot) * og for e, og in zip(es, outs)])
    o_ref[...] = o.astype(o_ref.dtype)


def _decode(qkv_s, caches, groups, biases, sinks):
    n_s = qkv_s.shape[0]
    ng = len(groups)
    q = qkv_s[:, :ng * ATTN_WIDTH].reshape(n_s, ng, N_HEADS, HEAD_DIM)
    kv = qkv_s[:, ng * ATTN_WIDTH:].reshape(n_s, 1, 2 * ng * KV_WIDTH)
    has_sink = sinks is not None
    in_specs = [pl.BlockSpec((None, ng, N_HEADS, HEAD_DIM), lambda b: (b, 0, 0, 0)),
                pl.BlockSpec((None, 1, 2 * ng * KV_WIDTH), lambda b: (b, 0, 0)),
                pl.BlockSpec((ng, N_HEADS, 2 * ATTN_BLOCK), lambda b: (0, 0, 0))]
    args = [q, kv, biases]
    if has_sink:
        in_specs.append(pl.BlockSpec((N_HEADS, 1), lambda b: (0, 0)))
        args.append(sinks.reshape(N_HEADS, 1))
    out_specs, out_shape = [], []
    for (win, _), c in zip(groups, caches):
        in_specs.append(pl.BlockSpec((None, 2, 2 * win, 128), lambda b: (b, 0, 0, 0)))
        args.append(c.reshape(n_s, 2, 2 * win, 128))
        out_specs.append(pl.BlockSpec((None, 2, 2 * win, 128), lambda b: (b, 0, 0, 0)))
        out_shape.append(jax.ShapeDtypeStruct((n_s, 2, 2 * win, 128), F32))
    out_specs.append(pl.BlockSpec((None, N_HEADS, HEAD_DIM), lambda b: (b, 0, 0)))
    out_shape.append(jax.ShapeDtypeStruct((n_s, N_HEADS, HEAD_DIM), BF16))
    res = pl.pallas_call(
        functools.partial(_decode_kernel, tuple(groups), has_sink),
        grid=(n_s,),
        in_specs=in_specs, out_specs=out_specs, out_shape=out_shape,
        compiler_params=_cparams("arbitrary"),
        name="decode_attention",
    )(*args)
    states = [r.reshape(n_s, 2, win, KV_HEADS, HEAD_DIM) for r, (win, _) in zip(res[:ng], groups)]
    return res[ng].reshape(n_s, ATTN_WIDTH), states


def _route(h, n_valid, wr_ref, br_ref, idx_ref, gate_ref, rank_ref, cnt_ref):
    tm = ROW_TILE
    logits = _dot_nt(wr_ref[...], h.astype(BF16)) + br_ref[...]
    e_iota = lax.broadcasted_iota(I32, (N_EXPERTS, tm), 0)
    valid = lax.broadcasted_iota(I32, (N_EXPERTS, tm), 1) < n_valid
    l = logits
    vals, sels = [], []
    for k in range(TOP_K):
        m = jnp.max(l, axis=0, keepdims=True)
        idx = jnp.min(jnp.where(l == m, e_iota, N_EXPERTS), axis=0, keepdims=True)
        sel = e_iota == idx
        idx_ref[pl.ds(k, 1), :] = idx
        vals.append(m)
        sels.append(sel & valid)
        l = jnp.where(sel, -jnp.inf, l)
    es = [jnp.exp(v - vals[0]) for v in vals]
    tot = es[0] + es[1] + es[2] + es[3]
    for k in range(TOP_K):
        gate_ref[pl.ds(k, 1), :] = es[k] / tot
    oh = jnp.concatenate([jnp.where(s, 1.0, 0.0) for s in sels], axis=0)
    r = lax.broadcasted_iota(I32, (tm, tm), 0)
    c = lax.broadcasted_iota(I32, (tm, tm), 1)
    upper = jnp.where(r < c, 1.0, 0.0).astype(BF16)
    prefix = _dot(oh.astype(BF16), upper)
    base = cnt_ref[:, 0:1]
    for k in range(TOP_K):
        oh_k = oh[k * N_EXPERTS:(k + 1) * N_EXPERTS]
        rank = jnp.sum(oh_k * (base + prefix[k * N_EXPERTS:(k + 1) * N_EXPERTS]), axis=0, keepdims=True)
        rank_ref[pl.ds(k, 1), :] = rank.astype(I32)
        base = base + jnp.sum(oh_k, axis=1, keepdims=True)
    cnt_ref[...] = jnp.broadcast_to(base, cnt_ref.shape)


def _router_kernel(n_ptiles, n_s, xp_ref, xs_ref, g_ref, shp_ref, scp_ref, shs_ref, scs_ref, wr_ref, br_ref,
                   h_ref, idx_ref, gate_ref, rank_ref, cnt_ref):
    t = pl.program_id(0)

    @pl.when(t == 0)
    def _():
        cnt_ref[...] = jnp.zeros(cnt_ref.shape, F32)

    @pl.when(t < n_ptiles)
    def _():
        h = _normmod(xp_ref[...], g_ref[...], shp_ref[...], scp_ref[...])
        h_ref[...] = h
        _route(h, ROW_TILE, wr_ref, br_ref, idx_ref, gate_ref, rank_ref, cnt_ref)

    @pl.when(t == n_ptiles)
    def _():
        hs = _normmod(xs_ref[...], g_ref[...], shs_ref[...], scs_ref[...])
        h = jnp.concatenate([hs, jnp.zeros((ROW_TILE - n_s, D_MODEL), F32)], axis=0)
        h_ref[...] = h
        _route(h, n_s, wr_ref, br_ref, idx_ref, gate_ref, rank_ref, cnt_ref)


def _router(xp, xs, gain, mods, w_router, b_router):
    n_p, n_s = xp.shape[0], xs.shape[0]
    n_pt = n_p // ROW_TILE
    n_pad = (n_pt + 1) * ROW_TILE
    seq_tiles = _seq_rows(xp) // ROW_TILE
    pt = lambda t: jnp.minimum(t, n_pt - 1)
    tok = pl.BlockSpec((TOP_K, ROW_TILE), lambda t: (0, t))
    return pl.pallas_call(
        functools.partial(_router_kernel, n_pt, n_s),
        grid=(n_pt + 1,),
        in_specs=[pl.BlockSpec((ROW_TILE, D_MODEL), lambda t: (pt(t), 0)),
                  pl.BlockSpec((n_s, D_MODEL), lambda t: (0, 0)),
                  pl.BlockSpec((1, D_MODEL), lambda t: (0, 0)),
                  mods.prompt_spec(3, D_MODEL, lambda t: pt(t) // seq_tiles),
                  mods.prompt_spec(4, D_MODEL, lambda t: pt(t) // seq_tiles),
                  mods.sample_spec(3, D_MODEL, n_s), mods.sample_spec(4, D_MODEL, n_s),
                  pl.BlockSpec((N_EXPERTS, D_MODEL), lambda t: (0, 0)),
                  pl.BlockSpec((N_EXPERTS, 1), lambda t: (0, 0))],
        out_specs=[pl.BlockSpec((ROW_TILE, D_MODEL), lambda t: (t, 0)), tok, tok, tok,
                   pl.BlockSpec((N_EXPERTS, 128), lambda t: (0, 0))],
        out_shape=[jax.ShapeDtypeStruct((n_pad, D_MODEL), F32),
                   jax.ShapeDtypeStruct((TOP_K, n_pad), I32),
                   jax.ShapeDtypeStruct((TOP_K, n_pad), F32),
                   jax.ShapeDtypeStruct((TOP_K, n_pad), I32),
                   jax.ShapeDtypeStruct((N_EXPERTS, 128), F32)],
        compiler_params=_cparams("arbitrary"),
        name="moe_router",
    )(xp, xs, gain, mods.m4, mods.m4, mods.m3, mods.m3,
      w_router.T.astype(BF16), b_router.reshape(N_EXPERTS, 1))


def _moe_expert_kernel(sb_e_ref, sb_ntile_ref, sb_nrow_ref,
                       dst_ref,
                       h_hbm,
                       wug_ref, wul_ref, bug_ref, bul_ref, wd_ref, bd_ref,
                       y_hbm,
                       xf_ref, xb_ref, yacc_ref, wub_ref, wdb_ref, gsem, osem):
    del sb_e_ref
    s = pl.program_id(0)
    f = pl.program_id(1)
    ntile = sb_ntile_ref[s]
    nrow = sb_nrow_ref[s]

    def gather_copy(r):
        tok = lax.shift_right_logical(dst_ref[0, 0, r], 2)
        return pltpu.make_async_copy(h_hbm.at[pl.ds(tok, 1)], xf_ref.at[pl.ds(r, 1)], gsem)

    def gather_tile_wait():
        return pltpu.make_async_copy(h_hbm.at[pl.ds(0, MOE_TM)], xf_ref.at[pl.ds(0, MOE_TM)], gsem)

    def out_copy(r):
        return pltpu.make_async_copy(yacc_ref.at[pl.ds(r, 1)], y_hbm.at[pl.ds(dst_ref[0, 0, r], 1)], osem)

    @pl.when(ntile > 0)
    def _active():
        @pl.when(f == 0)
        def _gather():
            def issue(r, c):
                gather_copy(r).start()
                return c
            lax.fori_loop(0, ntile * MOE_TM, issue, 0)

            def wait(i, c):
                gather_tile_wait().wait()
                return c
            lax.fori_loop(0, ntile, wait, 0)

            def cast(i, c):
                r0 = pl.multiple_of(i * MOE_TM, MOE_TM)
                xb_ref[pl.ds(r0, MOE_TM), :] = xf_ref[pl.ds(r0, MOE_TM), :].astype(BF16)
                return c
            lax.fori_loop(0, ntile, cast, 0)

        wub_ref[:, :MOE_TF] = wug_ref[...].astype(BF16)
        wub_ref[:, MOE_TF:] = wul_ref[...].astype(BF16)
        wdb_ref[...] = wd_ref[...].astype(BF16)

        def tile(i, c):
            r0 = pl.multiple_of(i * MOE_TM, MOE_TM)
            hh = _dot(xb_ref[pl.ds(r0, MOE_TM), :], wub_ref[...])
            hg = jnp.minimum(hh[:, :MOE_TF] + bug_ref[...], SWIGLU_LIMIT)
            hl = jnp.clip(hh[:, MOE_TF:] + bul_ref[...], -SWIGLU_LIMIT, SWIGLU_LIMIT)
            act = hg * (1.0 / (1.0 + jnp.exp(-SWIGLU_ALPHA * hg))) * (hl + 1.0)
            y = _dot(act.astype(BF16), wdb_ref[...])

            @pl.when(f == 0)
            def _():
                yacc_ref[pl.ds(r0, MOE_TM), :] = y + bd_ref[...]

            @pl.when(f > 0)
            def _():
                yacc_ref[pl.ds(r0, MOE_TM), :] += y
            return c
        lax.fori_loop(0, ntile, tile, 0)

        @pl.when(f == MOE_NF - 1)
        def _writeback():
            def start(r, c):
                out_copy(r).start()
                return c
            lax.fori_loop(0, nrow, start, 0)

            def wait_tile(i, c):
                pltpu.make_async_copy(yacc_ref.at[pl.ds(0, MOE_TM)], y_hbm.at[pl.ds(0, MOE_TM)], osem).wait()
                return c
            lax.fori_loop(0, nrow // MOE_TM, wait_tile, 0)

            def wait_row(r, c):
                out_copy(r).wait()
                return c
            lax.fori_loop(0, nrow % MOE_TM, wait_row, 0)


def _moe_n_tiles(nk):
    return (nk + N_EXPERTS * (MOE_TM - 1) + MOE_TM - 1) // MOE_TM


def _moe_n_superblocks(nk):
    return (_moe_n_tiles(nk) + N_EXPERTS * (MOE_SB_TILES - 1) + MOE_SB_TILES - 1) // MOE_SB_TILES


def _moe_plan(counts, nk):
    n_sb = _moe_n_superblocks(nk)
    tiles = (counts + MOE_TM - 1) // MOE_TM
    tile_end = jnp.cumsum(tiles)
    tile_start = tile_end - tiles
    nsb = (tiles + MOE_SB_TILES - 1) // MOE_SB_TILES
    sb_end = jnp.cumsum(nsb)
    sb_start = sb_end - nsb
    total_sb = sb_end[-1]
    s = jnp.arange(n_sb, dtype=I32)
    s_eff = jnp.minimum(s, total_sb - 1)
    e = jnp.minimum(jnp.searchsorted(sb_end, s_eff, side='right'), N_EXPERTS - 1).astype(I32)
    local = s_eff - sb_start[e]
    tile0 = tile_start[e] + local * MOE_SB_TILES
    live = s < total_sb
    ntile = jnp.where(live, jnp.clip(tiles[e] - local * MOE_SB_TILES, 0, MOE_SB_TILES), 0)
    nrow = jnp.where(live, jnp.clip(counts[e] - local * MOE_SB_ROWS, 0, MOE_SB_ROWS), 0)
    return tile_start.astype(I32), e, tile0.astype(I32), ntile.astype(I32), nrow.astype(I32)


def _moe_experts(h, idx, rank, counts, w_up, b_up, w_down, b_down, n_tok):
    n_pad = h.shape[0]
    nk = n_tok * TOP_K
    tile_start, sb_e, sb_tile0, sb_ntile, sb_nrow = _moe_plan(counts, nk)
    n_slots = _moe_n_tiles(nk) * MOE_TM
    n_sb = sb_e.shape[0]
    slot = tile_start[idx[:, :n_tok]] * MOE_TM + rank[:, :n_tok]
    dst = jnp.arange(n_tok, dtype=I32)[None, :] * TOP_K + jnp.arange(TOP_K, dtype=I32)[:, None]
    slot_dst = jnp.zeros((n_slots,), I32).at[slot.reshape(-1)].set(dst.reshape(-1))
    win = sb_tile0[:, None] * MOE_TM + jnp.arange(MOE_SB_ROWS, dtype=I32)[None, :]
    sb_dst = slot_dst[jnp.minimum(win, n_slots - 1)].reshape(n_sb, 1, MOE_SB_ROWS)

    def ff(s, f, nt):
        return jnp.where(nt[s] > 0, f, MOE_NF - 1)

    grid_spec = pltpu.PrefetchScalarGridSpec(
        num_scalar_prefetch=3,
        grid=(n_sb, MOE_NF),
        in_specs=[
            pl.BlockSpec((1, 1, MOE_SB_ROWS), lambda s, f, *_: (s, 0, 0), memory_space=pltpu.SMEM),
            pl.BlockSpec(memory_space=pl.ANY),
            pl.BlockSpec((None, D_MODEL, MOE_TF), lambda s, f, e, nt, nr: (e[s], 0, ff(s, f, nt))),
            pl.BlockSpec((None, D_MODEL, MOE_TF), lambda s, f, e, nt, nr: (e[s], 0, MOE_NF + ff(s, f, nt))),
            pl.BlockSpec((None, 1, MOE_TF), lambda s, f, e, nt, nr: (e[s], 0, ff(s, f, nt))),
            pl.BlockSpec((None, 1, MOE_TF), lambda s, f, e, nt, nr: (e[s], 0, MOE_NF + ff(s, f, nt))),
            pl.BlockSpec((None, MOE_TF, D_MODEL), lambda s, f, e, nt, nr: (e[s], ff(s, f, nt), 0)),
            pl.BlockSpec((None, 1, D_MODEL), lambda s, f, e, nt, nr: (e[s], 0, 0)),
        ],
        out_specs=pl.BlockSpec(memory_space=pl.ANY),
        scratch_shapes=[
            pltpu.VMEM((MOE_SB_ROWS, D_MODEL), F32),
            pltpu.VMEM((MOE_SB_ROWS, D_MODEL), BF16),
            pltpu.VMEM((MOE_SB_ROWS, D_MODEL), F32),
            pltpu.VMEM((D_MODEL, 2 * MOE_TF), BF16),
            pltpu.VMEM((MOE_TF, D_MODEL), BF16),
            pltpu.SemaphoreType.DMA(()),
            pltpu.SemaphoreType.DMA(()),
        ],
    )
    return pl.pallas_call(
        _moe_expert_kernel,
        grid_spec=grid_spec,
        out_shape=jax.ShapeDtypeStruct((n_pad * TOP_K, D_MODEL), F32),
        compiler_params=_cparams("arbitrary", "arbitrary"),
        name="moe_experts",
    )(sb_e, sb_ntile, sb_nrow, sb_dst, h,
      w_up, w_up, b_up.reshape(N_EXPERTS, 1, 2 * D_FF), b_up.reshape(N_EXPERTS, 1, 2 * D_FF),
      w_down, b_down.reshape(N_EXPERTS, 1, D_MODEL))


def _moe_combine_kernel(n_ptiles, n_s, xp_ref, xs_ref, y_ref, g_ref, gp_ref, gs_ref, op_ref, os_ref):
    t = pl.program_id(0)

    def mix(rows):
        acc = None
        for k in range(TOP_K):
            term = g_ref[0:rows, k:k + 1] * y_ref[0:rows, k * D_MODEL:(k + 1) * D_MODEL]
            acc = term if acc is None else acc + term
        return acc

    @pl.when(t < n_ptiles)
    def _():
        op_ref[...] = xp_ref[...] + gp_ref[...] * mix(ROW_TILE)

    @pl.when(t == n_ptiles)
    def _():
        os_ref[...] = xs_ref[...] + gs_ref[...] * mix(n_s)


def _moe_combine(xp, xs, y, gates_t, mods):
    n_p, n_s = xp.shape[0], xs.shape[0]
    n_pt = n_p // ROW_TILE
    seq_tiles = _seq_rows(xp) // ROW_TILE
    pt = lambda t: jnp.minimum(t, n_pt - 1)
    return pl.pallas_call(
        functools.partial(_moe_combine_kernel, n_pt, n_s),
        grid=(n_pt + 1,),
        in_specs=[pl.BlockSpec((ROW_TILE, D_MODEL), lambda t: (pt(t), 0)),
                  pl.BlockSpec((n_s, D_MODEL), lambda t: (0, 0)),
                  pl.BlockSpec((ROW_TILE, TOP_K * D_MODEL), lambda t: (t, 0)),
                  pl.BlockSpec((ROW_TILE, TOP_K), lambda t: (t, 0)),
                  mods.prompt_spec(5, D_MODEL, lambda t: pt(t) // seq_tiles),
                  mods.sample_spec(5, D_MODEL, n_s)],
        out_specs=[pl.BlockSpec((ROW_TILE, D_MODEL), lambda t: (pt(t), 0)),
                   pl.BlockSpec((n_s, D_MODEL), lambda t: (0, 0))],
        out_shape=[jax.ShapeDtypeStruct((n_p, D_MODEL), F32), jax.ShapeDtypeStruct((n_s, D_MODEL), F32)],
        compiler_params=_cparams("arbitrary"),
        name="moe_combine",
    )(xp, xs, y.reshape(-1, TOP_K * D_MODEL), gates_t, mods.m4, mods.m3)


def _moe_layer(xp, xs, gain, mods, w_router, b_router, w_up, b_up, w_down, b_down):
    n_tok = xp.shape[0] + xs.shape[0]
    h, idx, gates, rank, cnt = _router(xp, xs, gain, mods, w_router, b_router)
    counts = cnt[:, 0].astype(I32)
    y = _moe_experts(h, idx, rank, counts, w_up, b_up, w_down, b_down, n_tok)
    return _moe_combine(xp, xs, y, gates.T, mods)


def _qk_columns(qk_gain, n_groups):
    nq, nk = n_groups * N_HEADS, n_groups * KV_HEADS
    gain = jnp.concatenate([jnp.tile(qk_gain[0] * HEAD_DIM ** -0.5, nq), jnp.tile(qk_gain[1], nk),
                            jnp.ones((nk * HEAD_DIM,), F32)])
    flag = jnp.concatenate([jnp.ones(((nq + nk) * HEAD_DIM,), F32), jnp.zeros((nk * HEAD_DIM,), F32)])
    return gain[None, :], flag[None, :]


def _prompt_state(qkv_p, batch, k_off, v_off, win):
    seq = qkv_p.shape[0] // batch
    w = min(win, seq)
    r = qkv_p.reshape(batch, seq, -1)[:, seq - w:]
    k = r[:, :, k_off:k_off + KV_WIDTH].reshape(batch, w, KV_HEADS, HEAD_DIM)
    v = r[:, :, v_off:v_off + KV_WIDTH].reshape(batch, w, KV_HEADS, HEAD_DIM)
    return jnp.stack([k, v], axis=1)


def kernel(x_prompt, x_sample, c_prompt, c_sample, cache_a_kv, cache_b1_kv, cache_b2_kv, cache_b3_kv,
           rel_bias, norm_gain, w_ada, b_ada, w_qkv_a, qk_gain_a, sinks_a, w_o_a,
           w_qkv_b, qk_gain_b, w_o_b, w_router, b_router, w_up, b_up, w_down, b_down):
    b, s, d = x_prompt.shape
    bd, t, _ = x_sample.shape
    xp = x_prompt.reshape(b * s, d)
    xs = x_sample.reshape(bd * t, d)
    c_all = jnp.concatenate([c_sample, c_prompt, jnp.zeros((N_MOD_ROWS - bd - b, d), F32)], axis=0)
    mods_all = _ada(c_all, w_ada, b_ada)
    bias_p = [_prompt_bias(rel_bias, dil) for _, dil in B_PATTERNS]
    bias_d = [_decode_bias(rel_bias, dil) for _, dil in B_PATTERNS]
    caches_b = (cache_b1_kv, cache_b2_kv, cache_b3_kv)
    a_p, a_s = [], []
    b_p = [[], [], []]
    b_s = [[], [], []]
    for i in range(DEPTH):
        mods = _Mods(mods_all, i)
        j = i // 2
        hp, hs = _normmod_rows(xp, xs, norm_gain[i, 0][None, :], mods, 0, 1)
        if i % 2 == 0:
            gain, flag = _qk_columns(qk_gain_a[j], 1)
            qkv_p, qkv_s = _mm_qkv(hp, hs, w_qkv_a[j], gain, flag, 1280)
            op = _prompt_attention(qkv_p, QKV_A, 0, ATTN_WIDTH, ATTN_WIDTH + KV_WIDTH, 1, bias_p[0],
                                   sinks_a[j], BF16, False)
            os_, st_s = _decode(qkv_s, [cache_a_kv[j]], ((WINDOW_A, 1),), bias_d[0][None], sinks_a[j])
            a_p.append(_prompt_state(qkv_p, b, ATTN_WIDTH, ATTN_WIDTH + KV_WIDTH, WINDOW_A))
            a_s.append(st_s[0])
            w_o = w_o_a[j]
        else:
            gain, flag = _qk_columns(qk_gain_b[j], N_GROUPS_B)
            qkv_p, qkv_s = _mm_qkv(hp, hs, w_qkv_b[j], gain, flag, 1536)
            k0 = N_GROUPS_B * ATTN_WIDTH
            v0 = k0 + N_GROUPS_B * KV_WIDTH
            outs, lses = [], []
            for gi, (win, dil) in enumerate(B_PATTERNS):
                o_g, l_g = _prompt_attention(qkv_p, QKV_B, gi * ATTN_WIDTH, k0 + gi * KV_WIDTH,
                                             v0 + gi * KV_WIDTH, dil, bias_p[gi], None, F32, True)
                outs.append(o_g)
                lses.append(l_g)
                b_p[gi].append(_prompt_state(qkv_p, b, k0 + gi * KV_WIDTH, v0 + gi * KV_WIDTH, win))
            op = _combine_groups(outs, lses)
            os_, st_s = _decode(qkv_s, [c[j] for c in caches_b], B_PATTERNS, jnp.stack(bias_d), None)
            for gi in range(N_GROUPS_B):
                b_s[gi].append(st_s[gi])
            w_o = w_o_b[j]
        xp, xs = _mm_resid(op, os_, w_o, xp, xs, mods, 2, 1024)
        xp, xs = _moe_layer(xp, xs, norm_gain[i, 1][None, :], mods, w_router[i], b_router[i],
                            w_up[i], b_up[i], w_down[i], b_down[i])
    return (xp.reshape(b, s, d), xs.reshape(bd, t, d),
            jnp.stack(a_p), jnp.stack(a_s),
            jnp.stack(b_p[0]), jnp.stack(b_s[0]),
            jnp.stack(b_p[1]), jnp.stack(b_s[1]),
            jnp.stack(b_p[2]), jnp.stack(b_s[2]))
```

```python
import functools
import math

import jax
import jax.numpy as jnp
import numpy as np
from jax import lax
from jax.experimental import pallas as pl
from jax.experimental.pallas import tpu as pltpu

D_MODEL = 2048
DEPTH = 4
HEAD_DIM = 64
N_HEADS = 32
KV_HEADS = 4
HEADS_PER_KV = N_HEADS // KV_HEADS
KV_WIDTH = KV_HEADS * HEAD_DIM
WINDOW_A = 128
B_PATTERNS = ((128, 1), (512, 4), (2048, 16))
N_GROUPS_B = 3
RES = 16
N_BUCKETS = 32
MAX_DISTANCE = 2048
N_EXPERTS = 32
TOP_K = 4
D_FF = 2048
SWIGLU_ALPHA = 1.702
SWIGLU_LIMIT = 7.0
ATTN_BLOCK = 128
NORM_EPS = 1e-6
NEG_INF = -1e30
ATTN_WIDTH = N_HEADS * HEAD_DIM
QKV_A = ATTN_WIDTH + 2 * KV_WIDTH
QKV_B = N_GROUPS_B * QKV_A

F32 = jnp.float32
BF16 = jnp.bfloat16
I32 = jnp.int32

VMEM_LIMIT_BYTES = 56 * 1024 * 1024
MXU_DIM = 256

N_MOD_ROWS = 40
MOD_PROMPT_ROW0 = 32
ROW_TILE = 256
MM_BM = 512
ADA_BN = 1024

MOE_TM = 256
MOE_SB_TILES = 5
MOE_SB_ROWS = MOE_TM * MOE_SB_TILES
MOE_TF = 512
MOE_NF = D_FF // MOE_TF
MOE_ROW_UNROLL = 8


def _cparams(*sem):
    return pltpu.CompilerParams(dimension_semantics=sem, vmem_limit_bytes=VMEM_LIMIT_BYTES)


def _dot(a, b):
    return jnp.dot(a, b, preferred_element_type=F32)


def _dot_nt(a, b):
    return lax.dot_general(a, b, (((1,), (1,)), ((), ())), preferred_element_type=F32)


def _ada_kernel(c_ref, w_ref, b_ref, o_ref):
    c = c_ref[...]
    a = (c / (1.0 + jnp.exp(-c))).astype(BF16)
    o_ref[...] = _dot(a, w_ref[...].astype(BF16)) + b_ref[...]


def _ada(c_all, w_ada, b_ada):
    n6 = w_ada.shape[-1]
    return pl.pallas_call(
        _ada_kernel,
        grid=(DEPTH, n6 // ADA_BN),
        in_specs=[pl.BlockSpec((N_MOD_ROWS, D_MODEL), lambda i, n: (0, 0)),
                  pl.BlockSpec((None, D_MODEL, ADA_BN), lambda i, n: (i, 0, n)),
                  pl.BlockSpec((None, 1, ADA_BN), lambda i, n: (i, 0, n))],
        out_specs=pl.BlockSpec((None, N_MOD_ROWS, ADA_BN), lambda i, n: (i, 0, n)),
        out_shape=jax.ShapeDtypeStruct((DEPTH, N_MOD_ROWS, n6), F32),
        compiler_params=_cparams("arbitrary", "arbitrary"),
        name="ada",
    )(c_all, w_ada, b_ada.reshape(DEPTH, 1, n6))


class _Mods:
    def __init__(self, mods, layer):
        self.m3 = mods
        self.m4 = mods.reshape(DEPTH, N_MOD_ROWS, 1, mods.shape[-1])
        self.layer = layer

    def prompt_spec(self, chunk, width, seq_of_step, col_of_step=None):
        i, per = self.layer, D_MODEL // width

        def imap(*g):
            col = 0 if col_of_step is None else col_of_step(*g)
            return (i, MOD_PROMPT_ROW0 + seq_of_step(*g), 0, chunk * per + col)
        return pl.BlockSpec((None, None, 1, width), imap)

    def sample_spec(self, chunk, width, n_rows, col_of_step=None):
        i, per = self.layer, D_MODEL // width

        def imap(*g):
            col = 0 if col_of_step is None else col_of_step(*g)
            return (i, 0, chunk * per + col)
        return pl.BlockSpec((None, n_rows, width), imap)


def _normmod(x, gain, shift, scale):
    y = x * lax.rsqrt(jnp.mean(x * x, axis=-1, keepdims=True) + NORM_EPS)
    return (y * gain) * (1.0 + scale) + shift


def _normmod_kernel(x_ref, g_ref, sh_ref, sc_ref, o_ref):
    o_ref[...] = _normmod(x_ref[...], g_ref[...], sh_ref[...], sc_ref[...]).astype(o_ref.dtype)


def _normmod_rows(xp, xs, gain, mods, c_shift, c_scale):
    n_p, n_s = xp.shape[0], xs.shape[0]
    seq_tiles = _seq_rows(xp) // ROW_TILE
    gspec = pl.BlockSpec((1, D_MODEL), lambda t: (0, 0))
    hp = pl.pallas_call(
        _normmod_kernel,
        grid=(n_p // ROW_TILE,),
        in_specs=[pl.BlockSpec((ROW_TILE, D_MODEL), lambda t: (t, 0)), gspec,
                  mods.prompt_spec(c_shift, D_MODEL, lambda t: t // seq_tiles),
                  mods.prompt_spec(c_scale, D_MODEL, lambda t: t // seq_tiles)],
        out_specs=pl.BlockSpec((ROW_TILE, D_MODEL), lambda t: (t, 0)),
        out_shape=jax.ShapeDtypeStruct((n_p, D_MODEL), BF16),
        compiler_params=_cparams("arbitrary"),
        name="normmod_prompt",
    )(xp, gain, mods.m4, mods.m4)
    hs = pl.pallas_call(
        _normmod_kernel,
        grid=(1,),
        in_specs=[pl.BlockSpec((n_s, D_MODEL), lambda t: (0, 0)), gspec,
                  mods.sample_spec(c_shift, D_MODEL, n_s), mods.sample_spec(c_scale, D_MODEL, n_s)],
        out_specs=pl.BlockSpec((n_s, D_MODEL), lambda t: (0, 0)),
        out_shape=jax.ShapeDtypeStruct((n_s, D_MODEL), BF16),
        compiler_params=_cparams("arbitrary"),
        name="normmod_sample",
    )(xs, gain, mods.m3, mods.m3)
    return hp, hs


_SEQ_ROWS = 2048


def _seq_rows(xp):
    return _SEQ_ROWS


def _head_sumsq(y):
    y2 = y * y
    hi = y2.astype(BF16)
    lo = (y2 - hi.astype(F32)).astype(BF16)
    r = lax.broadcasted_iota(I32, (MXU_DIM, MXU_DIM), 0) // HEAD_DIM
    c = lax.broadcasted_iota(I32, (MXU_DIM, MXU_DIM), 1) // HEAD_DIM
    seg = jnp.where(r == c, 1.0, 0.0).astype(BF16)
    outs = []
    for j in range(y.shape[1] // MXU_DIM):
        sl = slice(j * MXU_DIM, (j + 1) * MXU_DIM)
        outs.append(_dot(hi[:, sl], seg) + _dot(lo[:, sl], seg))
    return jnp.concatenate(outs, axis=1)


def _qkv_epilogue(acc, gain, flag):
    inv = lax.rsqrt(_head_sumsq(acc) * (1.0 / HEAD_DIM) + NORM_EPS)
    return jnp.where(flag > 0.5, (acc * inv) * gain, acc)


def _mm_qkv_kernel(n_ptiles, ap_ref, as_ref, w_ref, gain_ref, flag_ref, op_ref, os_ref, wb_ref):
    t = pl.program_id(1)

    @pl.when(t == 0)
    def _():
        wb_ref[...] = w_ref[...].astype(BF16)

    @pl.when(t < n_ptiles)
    def _():
        op_ref[...] = _qkv_epilogue(_dot(ap_ref[...], wb_ref[...]), gain_ref[...], flag_ref[...])

    @pl.when(t == n_ptiles)
    def _():
        os_ref[...] = _qkv_epilogue(_dot(as_ref[...], wb_ref[...]), gain_ref[...], flag_ref[...])


def _mm_qkv(hp, hs, w, layer, col_gain, col_flag, bn):
    n_p, n_s, n = hp.shape[0], hs.shape[0], w.shape[2]
    n_pt = n_p // MM_BM
    pt = lambda c, t: (jnp.minimum(t, n_pt - 1), 0)
    return pl.pallas_call(
        functools.partial(_mm_qkv_kernel, n_pt),
        grid=(n // bn, n_pt + 1),
        in_specs=[pl.BlockSpec((MM_BM, D_MODEL), pt),
                  pl.BlockSpec((n_s, D_MODEL), lambda c, t: (0, 0)),
                  pl.BlockSpec((None, D_MODEL, bn), lambda c, t: (layer, 0, c)),
                  pl.BlockSpec((1, bn), lambda c, t: (0, c)),
                  pl.BlockSpec((1, bn), lambda c, t: (0, c))],
        out_specs=[pl.BlockSpec((MM_BM, bn), lambda c, t: (jnp.minimum(t, n_pt - 1), c)),
                   pl.BlockSpec((n_s, bn), lambda c, t: (0, c))],
        out_shape=[jax.ShapeDtypeStruct((n_p, n), F32), jax.ShapeDtypeStruct((n_s, n), F32)],
        scratch_shapes=[pltpu.VMEM((D_MODEL, bn), BF16)],
        compiler_params=_cparams("arbitrary", "arbitrary"),
        name="mm_qkv",
    )(hp, hs, w, col_gain, col_flag)


def _mm_resid_kernel(n_ptiles, ap_ref, as_ref, w_ref, xp_ref, xs_ref, gp_ref, gs_ref, op_ref, os_ref, wb_ref):
    t = pl.program_id(1)

    @pl.when(t == 0)
    def _():
        wb_ref[...] = w_ref[...].astype(BF16)

    @pl.when(t < n_ptiles)
    def _():
        op_ref[...] = xp_ref[...] + gp_ref[...] * _dot(ap_ref[...].astype(BF16), wb_ref[...])

    @pl.when(t == n_ptiles)
    def _():
        os_ref[...] = xs_ref[...] + gs_ref[...] * _dot(as_ref[...], wb_ref[...])


def _mm_resid(ap, as_, w, layer, xp, xs, mods, c_gate, bn):
    n_p, n_s, k, n = ap.shape[0], as_.shape[0], w.shape[1], w.shape[2]
    n_pt = n_p // MM_BM
    seq_tiles = _seq_rows(xp) // MM_BM
    prow = lambda c, t: jnp.minimum(t, n_pt - 1)
    return pl.pallas_call(
        functools.partial(_mm_resid_kernel, n_pt),
        grid=(n // bn, n_pt + 1),
        in_specs=[pl.BlockSpec((MM_BM, k), lambda c, t: (prow(c, t), 0)),
                  pl.BlockSpec((n_s, k), lambda c, t: (0, 0)),
                  pl.BlockSpec((None, k, bn), lambda c, t: (layer, 0, c)),
                  pl.BlockSpec((MM_BM, bn), lambda c, t: (prow(c, t), c)),
                  pl.BlockSpec((n_s, bn), lambda c, t: (0, c)),
                  mods.prompt_spec(c_gate, bn, lambda c, t: prow(c, t) // seq_tiles, lambda c, t: c),
                  mods.sample_spec(c_gate, bn, n_s, lambda c, t: c)],
        out_specs=[pl.BlockSpec((MM_BM, bn), lambda c, t: (prow(c, t), c)),
                   pl.BlockSpec((n_s, bn), lambda c, t: (0, c))],
        out_shape=[jax.ShapeDtypeStruct((n_p, n), F32), jax.ShapeDtypeStruct((n_s, n), F32)],
        scratch_shapes=[pltpu.VMEM((k, bn), BF16)],
        compiler_params=_cparams("arbitrary", "arbitrary"),
        name="mm_resid",
    )(ap, as_, w, xp, xs, mods.m4, mods.m3)


def _attn_kernel(has_sink, with_lse, *refs):
    if has_sink:
        sink_ref, refs = refs[0], refs[1:]
    q_refs = refs[:KV_HEADS]
    kc_ref, kp_ref, vc_ref, vp_ref, bias_ref = refs[KV_HEADS:KV_HEADS + 5]
    o_ref = refs[KV_HEADS + 5]
    lse_ref = refs[KV_HEADS + 6] if with_lse else None

    def rows(ref):
        return ref[...].reshape(ATTN_BLOCK, ref.shape[-1])

    kw = jnp.concatenate([rows(kp_ref), rows(kc_ref)], axis=0).astype(BF16)
    vw = jnp.concatenate([rows(vp_ref), rows(vc_ref)], axis=0).astype(BF16)
    lane = lax.broadcasted_iota(I32, (ATTN_BLOCK, 128), 1)
    lse_all = jnp.zeros((ATTN_BLOCK, 128), F32)
    for kh in range(KV_HEADS):
        k_h = kw[:, kh * HEAD_DIM:(kh + 1) * HEAD_DIM]
        q_all = rows(q_refs[kh]).astype(BF16)
        outs = []
        for g in range(HEADS_PER_KV):
            h = kh * HEADS_PER_KV + g
            s = _dot_nt(q_all[:, g * HEAD_DIM:(g + 1) * HEAD_DIM], k_h) + bias_ref[h]
            m = jnp.max(s, axis=-1, keepdims=True)
            if has_sink:
                m = jnp.maximum(m, sink_ref[h])
            p = jnp.exp(s - m)
            den = jnp.sum(p, axis=-1, keepdims=True)
            if has_sink:
                den = den + jnp.exp(sink_ref[h] - m)
            o_full = _dot(p.astype(BF16), vw)
            outs.append(o_full[:, kh * HEAD_DIM:(kh + 1) * HEAD_DIM] / den)
            if with_lse:
                lse_all = jnp.where(lane == h, m + jnp.log(den), lse_all)
        lo, w = kh * HEADS_PER_KV * HEAD_DIM, HEADS_PER_KV * HEAD_DIM
        o_kh = jnp.concatenate(outs, axis=1).astype(o_ref.dtype)
        o_ref[..., lo:lo + w] = o_kh.reshape(o_ref.shape[:-1] + (w,))
    if with_lse:
        lse_ref[...] = lse_all.reshape(lse_ref.shape)


def _prompt_attention(qkv_p, width, q_off, k_off, v_off, dil, bias, sinks, out_dtype, with_lse):
    n_p = qkv_p.shape[0]
    batch = n_p // _seq_rows(qkv_p)
    m_per = _seq_rows(qkv_p) // RES
    rr = RES // 4
    nb = RES // dil
    qw = HEADS_PER_KV * HEAD_DIM
    has_sink = sinks is not None

    def view(a):
        return a.reshape(batch, rr, 4, m_per, a.shape[-1])

    def spec(cols, col_block, prev):
        def blk(j):
            return jnp.maximum(j - 1, 0) if prev else j
        if dil == 1:
            return pl.BlockSpec((None, rr, 4, ATTN_BLOCK // RES, cols), lambda b, j, r: (b, 0, 0, blk(j), col_block))
        if dil == 4:
            return pl.BlockSpec((None, rr, None, ATTN_BLOCK // rr, cols), lambda b, j, r: (b, 0, r, blk(j), col_block))
        return pl.BlockSpec((None, None, None, ATTN_BLOCK, cols), lambda b, j, r: (b, r // 4, r % 4, 0, col_block))

    in_specs = [spec(qw, q_off // qw + kh, False) for kh in range(KV_HEADS)]
    in_specs += [spec(KV_WIDTH, k_off // KV_WIDTH, False), spec(KV_WIDTH, k_off // KV_WIDTH, True),
                 spec(KV_WIDTH, v_off // KV_WIDTH, False), spec(KV_WIDTH, v_off // KV_WIDTH, True),
                 pl.BlockSpec((None, N_HEADS, ATTN_BLOCK, 2 * ATTN_BLOCK),
                              lambda b, j, r: (jnp.where(j == 0, 1, 0), 0, 0, 0))]
    args = [view(qkv_p)] * (KV_HEADS + 4) + [bias]
    if has_sink:
        in_specs = [pl.BlockSpec(memory_space=pltpu.SMEM)] + in_specs
        args = [sinks] + args
    out_specs = [spec(ATTN_WIDTH, 0, False)]
    out_shape = [jax.ShapeDtypeStruct((batch, rr, 4, m_per, ATTN_WIDTH), out_dtype)]
    if with_lse:
        out_specs.append(spec(128, 0, False))
        out_shape.append(jax.ShapeDtypeStruct((batch, rr, 4, m_per, 128), F32))
    res = pl.pallas_call(
        functools.partial(_attn_kernel, has_sink, with_lse),
        grid=(batch, nb, dil),
        in_specs=in_specs, out_specs=out_specs, out_shape=out_shape,
        compiler_params=_cparams("arbitrary", "arbitrary", "arbitrary"),
        name="prompt_attention",
    )(*args)
    o = res[0].reshape(n_p, ATTN_WIDTH)
    if with_lse:
        return o, res[1].reshape(n_p, 128)
    return o


def _block_offsets(dil):
    i = np.arange(ATTN_BLOCK)
    if dil == 1:
        return (i % (ATTN_BLOCK // RES)) * RES + i // (ATTN_BLOCK // RES)
    if dil == 4:
        per = ATTN_BLOCK // (RES // 4)
        return (i % per) * (RES // 4) + i // per
    return i


def _t5_bucket_static(dist):
    dist = np.maximum(np.asarray(dist, np.int64), 0)
    max_exact = N_BUCKETS // 2
    d = dist.astype(np.float32)
    ratio = np.log(np.maximum(d, np.float32(1.0)) / np.float32(max_exact)) / np.float32(
        math.log(MAX_DISTANCE / max_exact))
    large = max_exact + (ratio.astype(np.float32) * np.float32(N_BUCKETS - max_exact)).astype(np.int32)
    large = np.minimum(large, N_BUCKETS - 1)
    return np.where(dist < max_exact, dist, large).astype(np.int32)


def _bias_lookup(rel_bias, bucket):
    onehot = jnp.asarray(bucket[None] == np.arange(N_BUCKETS).reshape((N_BUCKETS,) + (1,) * bucket.ndim))
    tab = rel_bias.astype(F32).reshape((N_BUCKETS, N_HEADS) + (1,) * bucket.ndim)
    return jnp.sum(jnp.where(onehot[:, None], tab, 0.0), axis=0)


def _prompt_bias(rel_bias, dil):
    off = _block_offsets(dil)
    q_pos = ATTN_BLOCK + off[:, None]
    k_pos = np.concatenate([off, ATTN_BLOCK + off])[None, :]
    dist = q_pos - k_pos
    bias = _bias_lookup(rel_bias, _t5_bucket_static(dist * dil))
    valid = (dist >= 0) & (dist <= ATTN_BLOCK)
    first = valid & (k_pos >= ATTN_BLOCK)
    return jnp.stack([jnp.where(jnp.asarray(valid)[None], bias, NEG_INF),
                      jnp.where(jnp.asarray(first)[None], bias, NEG_INF)])


def _decode_bias(rel_bias, dil):
    steps = np.concatenate([ATTN_BLOCK - np.arange(ATTN_BLOCK), [0], np.zeros(ATTN_BLOCK - 1, np.int64)])
    b = _bias_lookup(rel_bias, _t5_bucket_static(dil * steps))
    return jnp.where(jnp.asarray(np.arange(2 * ATTN_BLOCK) <= ATTN_BLOCK)[None], b, 0.0)


def _bcast_heads(w):
    r = lax.broadcasted_iota(I32, (128, ATTN_WIDTH), 0)
    c = lax.broadcasted_iota(I32, (128, ATTN_WIDTH), 1) // HEAD_DIM
    e = jnp.where(r == c, 1.0, 0.0).astype(BF16)
    hi = w.astype(BF16)
    r1 = w - hi.astype(F32)
    mid = r1.astype(BF16)
    lo = (r1 - mid.astype(F32)).astype(BF16)
    return _dot(hi, e) + _dot(mid, e) + _dot(lo, e)


def _combine_kernel(o1_ref, o2_ref, o3_ref, l1_ref, l2_ref, l3_ref, o_ref):
    l1, l2, l3 = l1_ref[...], l2_ref[...], l3_ref[...]
    m = jnp.maximum(jnp.maximum(l1, l2), l3)
    e1, e2, e3 = jnp.exp(l1 - m), jnp.exp(l2 - m), jnp.exp(l3 - m)
    inv = 1.0 / (e1 + e2 + e3)
    acc = _bcast_heads(e1 * inv) * o1_ref[...]
    acc = acc + _bcast_heads(e2 * inv) * o2_ref[...]
    acc = acc + _bcast_heads(e3 * inv) * o3_ref[...]
    o_ref[...] = acc.astype(o_ref.dtype)


def _combine_groups(outs, lses):
    n_p = outs[0].shape[0]
    ospec = pl.BlockSpec((ROW_TILE, ATTN_WIDTH), lambda t: (t, 0))
    lspec = pl.BlockSpec((ROW_TILE, 128), lambda t: (t, 0))
    return pl.pallas_call(
        _combine_kernel,
        grid=(n_p // ROW_TILE,),
        in_specs=[ospec] * 3 + [lspec] * 3,
        out_specs=ospec,
        out_shape=jax.ShapeDtypeStruct((n_p, ATTN_WIDTH), BF16),
        compiler_params=_cparams("arbitrary"),
        name="combine_groups",
    )(*outs, *lses)


def _decode_kernel(groups, has_sink, *refs):
    ng = len(groups)
    q_ref, kv_ref, bias_ref = refs[:3]
    pos = 3
    if has_sink:
        sink_ref = refs[pos]
        pos += 1
    cache_refs = refs[pos:pos + ng]
    state_refs = refs[pos + ng:pos + 2 * ng]
    o_ref = refs[pos + 2 * ng]

    row_kh = lax.broadcasted_iota(I32, (N_HEADS, HEAD_DIM), 0) // HEADS_PER_KV
    outs, lses = [], []
    for g, (win, dil) in enumerate(groups):
        c_ref, st_ref = cache_refs[g], state_refs[g]
        k_new = kv_ref[:, g * KV_WIDTH:(g + 1) * KV_WIDTH]
        v_new = kv_ref[:, (ng + g) * KV_WIDTH:(ng + g + 1) * KV_WIDTH]
        for kv, new in ((0, k_new), (1, v_new)):
            st_ref[kv, pl.ds(0, 2 * win - 2), :] = c_ref[kv, pl.ds(2, 2 * win - 2), :]
            st_ref[kv, pl.ds(2 * win - 2, 1), :] = new[:, :128]
            st_ref[kv, pl.ds(2 * win - 1, 1), :] = new[:, 128:]
        k_sel = jnp.concatenate([c_ref[0, pl.ds(half, ATTN_BLOCK, stride=2 * dil), :] for half in (0, 1)], axis=1)
        v_sel = jnp.concatenate([c_ref[1, pl.ds(half, ATTN_BLOCK, stride=2 * dil), :] for half in (0, 1)], axis=1)
        q = q_ref[g]
        q_exp = jnp.concatenate([jnp.where(row_kh == kh, q, 0.0) for kh in range(KV_HEADS)], axis=1)
        qb = q_exp.astype(BF16)
        bias = bias_ref[g]
        s = _dot_nt(qb, k_sel.astype(BF16)) + bias[:, :ATTN_BLOCK]
        s0 = jnp.sum(qb.astype(F32) * k_new.astype(BF16).astype(F32), axis=-1, keepdims=True)
        s0 = s0 + bias[:, ATTN_BLOCK:ATTN_BLOCK + 1]
        m = jnp.maximum(jnp.max(s, axis=-1, keepdims=True), s0)
        if has_sink:
            m = jnp.maximum(m, sink_ref[...])
        p = jnp.exp(s - m)
        p0 = jnp.exp(s0 - m)
        den = jnp.sum(p, axis=-1, keepdims=True) + p0
        if has_sink:
            den = den + jnp.exp(sink_ref[...] - m)
        o_full = _dot(p.astype(BF16), v_sel.astype(BF16))
        o_full = o_full + p0.astype(BF16).astype(F32) * v_new.astype(BF16).astype(F32)
        o_h = jnp.zeros((N_HEADS, HEAD_DIM), F32)
        for kh in range(KV_HEADS):
            o_h = o_h + jnp.where(row_kh == kh, o_full[:, kh * HEAD_DIM:(kh + 1) * HEAD_DIM], 0.0)
        outs.append(o_h / den)
        lses.append(m + jnp.log(den))
    if ng == 1:
        o = outs[0]
    else:
        m = functools.reduce(jnp.maximum, lses)
        es = [jnp.exp(l - m) for l in lses]
        tot = functools.reduce(lambda a, b: a + b, es)
        o = functools.reduce(lambda a, b: a + b, [(e / tot) * og for e, og in zip(es, outs)])
    o_ref[...] = o.astype(o_ref.dtype)


def _decode(qkv_s, caches, layer, groups, biases, sinks):
    n_s = qkv_s.shape[0]
    ng = len(groups)
    q = qkv_s[:, :ng * ATTN_WIDTH].reshape(n_s, ng, N_HEADS, HEAD_DIM)
    kv = qkv_s[:, ng * ATTN_WIDTH:].reshape(n_s, 1, 2 * ng * KV_WIDTH)
    has_sink = sinks is not None
    in_specs = [pl.BlockSpec((None, ng, N_HEADS, HEAD_DIM), lambda b: (b, 0, 0, 0)),
                pl.BlockSpec((None, 1, 2 * ng * KV_WIDTH), lambda b: (b, 0, 0)),
                pl.BlockSpec((ng, N_HEADS, 2 * ATTN_BLOCK), lambda b: (0, 0, 0))]
    args = [q, kv, biases]
    if has_sink:
        in_specs.append(pl.BlockSpec((N_HEADS, 1), lambda b: (0, 0)))
        args.append(sinks.reshape(N_HEADS, 1))
    out_specs, out_shape = [], []
    for (win, _), c in zip(groups, caches):
        in_specs.append(pl.BlockSpec((None, None, 2, 2 * win, 128), lambda b: (layer, b, 0, 0, 0)))
        args.append(c.reshape(c.shape[0], n_s, 2, 2 * win, 128))
        out_specs.append(pl.BlockSpec((None, 2, 2 * win, 128), lambda b: (b, 0, 0, 0)))
        out_shape.append(jax.ShapeDtypeStruct((n_s, 2, 2 * win, 128), F32))
    out_specs.append(pl.BlockSpec((None, N_HEADS, HEAD_DIM), lambda b: (b, 0, 0)))
    out_shape.append(jax.ShapeDtypeStruct((n_s, N_HEADS, HEAD_DIM), BF16))
    res = pl.pallas_call(
        functools.partial(_decode_kernel, tuple(groups), has_sink),
        grid=(n_s,),
        in_specs=in_specs, out_specs=out_specs, out_shape=out_shape,
        compiler_params=_cparams("arbitrary"),
        name="decode_attention",
    )(*args)
    states = [r.reshape(n_s, 2, win, KV_HEADS, HEAD_DIM) for r, (win, _) in zip(res[:ng], groups)]
    return res[ng].reshape(n_s, ATTN_WIDTH), states


def _route(h, n_valid, wr_ref, br_ref, idx_ref, gate_ref, rank_ref, cnt_ref):
    tm = ROW_TILE
    logits = _dot_nt(wr_ref[...], h.astype(BF16)) + br_ref[...]
    e_iota = lax.broadcasted_iota(I32, (N_EXPERTS, tm), 0)
    valid = lax.broadcasted_iota(I32, (N_EXPERTS, tm), 1) < n_valid
    l = logits
    vals, sels = [], []
    for k in range(TOP_K):
        m = jnp.max(l, axis=0, keepdims=True)
        idx = jnp.min(jnp.where(l == m, e_iota, N_EXPERTS), axis=0, keepdims=True)
        sel = e_iota == idx
        idx_ref[pl.ds(k, 1), :] = idx
        vals.append(m)
        sels.append(sel & valid)
        l = jnp.where(sel, -jnp.inf, l)
    es = [jnp.exp(v - vals[0]) for v in vals]
    tot = es[0] + es[1] + es[2] + es[3]
    for k in range(TOP_K):
        gate_ref[pl.ds(k, 1), :] = es[k] / tot
    oh = jnp.concatenate([jnp.where(s, 1.0, 0.0) for s in sels], axis=0)
    r = lax.broadcasted_iota(I32, (tm, tm), 0)
    c = lax.broadcasted_iota(I32, (tm, tm), 1)
    upper = jnp.where(r < c, 1.0, 0.0).astype(BF16)
    prefix = _dot(oh.astype(BF16), upper)
    base = cnt_ref[:, 0:1]
    for k in range(TOP_K):
        oh_k = oh[k * N_EXPERTS:(k + 1) * N_EXPERTS]
        rank = jnp.sum(oh_k * (base + prefix[k * N_EXPERTS:(k + 1) * N_EXPERTS]), axis=0, keepdims=True)
        rank_ref[pl.ds(k, 1), :] = rank.astype(I32)
        base = base + jnp.sum(oh_k, axis=1, keepdims=True)
    cnt_ref[...] = jnp.broadcast_to(base, cnt_ref.shape)


def _router_kernel(n_ptiles, n_s, xp_ref, xs_ref, g_ref, shp_ref, scp_ref, shs_ref, scs_ref, wr_ref, br_ref,
                   h_ref, idx_ref, gate_ref, rank_ref, cnt_ref):
    t = pl.program_id(0)

    @pl.when(t == 0)
    def _():
        cnt_ref[...] = jnp.zeros(cnt_ref.shape, F32)

    @pl.when(t < n_ptiles)
    def _():
        h = _normmod(xp_ref[...], g_ref[...], shp_ref[...], scp_ref[...])
        h_ref[...] = h
        _route(h, ROW_TILE, wr_ref, br_ref, idx_ref, gate_ref, rank_ref, cnt_ref)

    @pl.when(t == n_ptiles)
    def _():
        hs = _normmod(xs_ref[...], g_ref[...], shs_ref[...], scs_ref[...])
        h = jnp.concatenate([hs, jnp.zeros((ROW_TILE - n_s, D_MODEL), F32)], axis=0)
        h_ref[...] = h
        _route(h, n_s, wr_ref, br_ref, idx_ref, gate_ref, rank_ref, cnt_ref)


def _router(xp, xs, gain, mods, w_router, b_router):
    n_p, n_s = xp.shape[0], xs.shape[0]
    n_pt = n_p // ROW_TILE
    n_pad = (n_pt + 1) * ROW_TILE
    seq_tiles = _seq_rows(xp) // ROW_TILE
    pt = lambda t: jnp.minimum(t, n_pt - 1)
    tok = pl.BlockSpec((TOP_K, ROW_TILE), lambda t: (0, t))
    return pl.pallas_call(
        functools.partial(_router_kernel, n_pt, n_s),
        grid=(n_pt + 1,),
        in_specs=[pl.BlockSpec((ROW_TILE, D_MODEL), lambda t: (pt(t), 0)),
                  pl.BlockSpec((n_s, D_MODEL), lambda t: (0, 0)),
                  pl.BlockSpec((1, D_MODEL), lambda t: (0, 0)),
                  mods.prompt_spec(3, D_MODEL, lambda t: pt(t) // seq_tiles),
                  mods.prompt_spec(4, D_MODEL, lambda t: pt(t) // seq_tiles),
                  mods.sample_spec(3, D_MODEL, n_s), mods.sample_spec(4, D_MODEL, n_s),
                  pl.BlockSpec((N_EXPERTS, D_MODEL), lambda t: (0, 0)),
                  pl.BlockSpec((N_EXPERTS, 1), lambda t: (0, 0))],
        out_specs=[pl.BlockSpec((ROW_TILE, D_MODEL), lambda t: (t, 0)), tok, tok, tok,
                   pl.BlockSpec((N_EXPERTS, 128), lambda t: (0, 0))],
        out_shape=[jax.ShapeDtypeStruct((n_pad, D_MODEL), F32),
                   jax.ShapeDtypeStruct((TOP_K, n_pad), I32),
                   jax.ShapeDtypeStruct((TOP_K, n_pad), F32),
                   jax.ShapeDtypeStruct((TOP_K, n_pad), I32),
                   jax.ShapeDtypeStruct((N_EXPERTS, 128), F32)],
        compiler_params=_cparams("arbitrary"),
        name="moe_router",
    )(xp, xs, gain, mods.m4, mods.m4, mods.m3, mods.m3,
      w_router.T.astype(BF16), b_router.reshape(N_EXPERTS, 1))


def _moe_expert_kernel(sb_e_ref, sb_tile0_ref, sb_ntile_ref,
                       *refs):
    del sb_e_ref, sb_tile0_ref
    dst_refs = refs[:MOE_SB_TILES]
    (h_hbm, wug_ref, wul_ref, bug_ref, bul_ref, wd_ref, bd_ref, y_hbm,
     xb_ref, yacc_ref, wub_ref, wdb_ref, gsem, osem) = refs[MOE_SB_TILES:]
    s = pl.program_id(0)
    f = pl.program_id(1)
    ntile = sb_ntile_ref[s]
    half = MOE_TM // 2

    def gather_copy(i, rr):
        tok = lax.shift_right_logical(dst_refs[i][0, 0, rr], 2)
        return pltpu.make_async_copy(h_hbm.at[pl.ds(tok, 1)], yacc_ref.at[pl.ds(i * MOE_TM + rr, 1)], gsem)

    def out_copy(i, rr):
        return pltpu.make_async_copy(yacc_ref.at[pl.ds(i * MOE_TM + rr, 1)],
                                     y_hbm.at[pl.ds(dst_refs[i][0, 0, rr], 1)], osem)

    def gather_tile_wait():
        return pltpu.make_async_copy(h_hbm.at[pl.ds(0, MOE_TM)], yacc_ref.at[pl.ds(0, MOE_TM)], gsem)

    def out_tile_wait():
        return pltpu.make_async_copy(yacc_ref.at[pl.ds(0, MOE_TM)], y_hbm.at[pl.ds(0, MOE_TM)], osem)

    def per_row(i, make_copy):
        def body(g, c):
            for u in range(MOE_ROW_UNROLL):
                make_copy(i, g * MOE_ROW_UNROLL + u).start()
            return c
        lax.fori_loop(0, MOE_TM // MOE_ROW_UNROLL, body, 0)

    @pl.when(ntile > 0)
    def _active():
        @pl.when(f == 0)
        def _gather():
            for i in range(MOE_SB_TILES):
                @pl.when(i < ntile)
                def _():
                    per_row(i, gather_copy)
            for i in range(MOE_SB_TILES):
                @pl.when(i < ntile)
                def _():
                    gather_tile_wait().wait()
            for i in range(MOE_SB_TILES):
                @pl.when(i < ntile)
                def _():
                    xb_ref[i * MOE_TM:(i + 1) * MOE_TM, :] = yacc_ref[i * MOE_TM:(i + 1) * MOE_TM, :].astype(BF16)
                    yacc_ref[i * MOE_TM:(i + 1) * MOE_TM, :] = jnp.broadcast_to(bd_ref[...], (MOE_TM, D_MODEL))

        wub_ref[:, :MOE_TF] = wug_ref[...].astype(BF16)
        wub_ref[:, MOE_TF:] = wul_ref[...].astype(BF16)
        wdb_ref[...] = wd_ref[...].astype(BF16)

        def tile(i, c):
            for hf in range(2):
                r0 = pl.multiple_of(i * MOE_TM + hf * half, half)
                hh = _dot(xb_ref[pl.ds(r0, half), :], wub_ref[...])
                hg = jnp.minimum(hh[:, :MOE_TF] + bug_ref[...], SWIGLU_LIMIT)
                hl = jnp.clip(hh[:, MOE_TF:] + bul_ref[...], -SWIGLU_LIMIT, SWIGLU_LIMIT)
                act = hg * (1.0 / (1.0 + jnp.exp(-SWIGLU_ALPHA * hg))) * (hl + 1.0)
                yacc_ref[pl.ds(r0, half), :] += _dot(act.astype(BF16), wdb_ref[...])
            return c
        lax.fori_loop(0, ntile, tile, 0)

        @pl.when(f == MOE_NF - 1)
        def _writeback():
            for i in range(MOE_SB_TILES):
                @pl.when(i < ntile)
                def _():
                    per_row(i, out_copy)
            for i in range(MOE_SB_TILES):
                @pl.when(i < ntile)
                def _():
                    out_tile_wait().wait()


def _moe_n_tiles(nk):
    return (nk + N_EXPERTS * (MOE_TM - 1) + MOE_TM - 1) // MOE_TM


def _moe_n_superblocks(nk):
    return (_moe_n_tiles(nk) + N_EXPERTS * (MOE_SB_TILES - 1) + MOE_SB_TILES - 1) // MOE_SB_TILES


def _moe_plan(counts, nk):
    n_sb = _moe_n_superblocks(nk)
    tiles = (counts + MOE_TM - 1) // MOE_TM
    tile_end = jnp.cumsum(tiles)
    tile_start = tile_end - tiles
    nsb = (tiles + MOE_SB_TILES - 1) // MOE_SB_TILES
    sb_end = jnp.cumsum(nsb)
    sb_start = sb_end - nsb
    total_sb = sb_end[-1]
    s = jnp.arange(n_sb, dtype=I32)
    s_eff = jnp.minimum(s, total_sb - 1)
    e = jnp.minimum(jnp.searchsorted(sb_end, s_eff, side='right'), N_EXPERTS - 1).astype(I32)
    local = s_eff - sb_start[e]
    tile0 = tile_start[e] + local * MOE_SB_TILES
    live = s < total_sb
    ntile = jnp.where(live, jnp.clip(tiles[e] - local * MOE_SB_TILES, 0, MOE_SB_TILES), 0)
    return tile_start.astype(I32), e, tile0.astype(I32), ntile.astype(I32)


def _moe_experts(h, idx, rank, counts, layer, w_up, b_up, w_down, b_down, n_tok):
    n_pad = h.shape[0]
    nk = n_tok * TOP_K
    tile_start, sb_e, sb_tile0, sb_ntile = _moe_plan(counts, nk)
    n_tiles = _moe_n_tiles(nk)
    n_slots = n_tiles * MOE_TM
    n_sb = sb_e.shape[0]
    assert (n_pad - n_tok) * TOP_K >= MOE_TM
    e_ids = jnp.arange(N_EXPERTS, dtype=I32)[:, None, None]
    start_of = jnp.sum(jnp.where(idx[None, :, :n_tok] == e_ids, tile_start[:, None, None], 0), axis=0)
    slot = start_of * MOE_TM + rank[:, :n_tok]
    dst = jnp.arange(n_tok, dtype=I32)[None, :] * TOP_K + jnp.arange(TOP_K, dtype=I32)[:, None]
    pad_dst = nk + jnp.arange(n_slots, dtype=I32) % MOE_TM
    slot_dst = pad_dst.at[slot.reshape(-1)].set(dst.reshape(-1)).reshape(n_tiles, 1, MOE_TM)

    def ff(s, f, nt):
        return jnp.where(nt[s] > 0, f, MOE_NF - 1)

    def dst_spec(i):
        return pl.BlockSpec((1, 1, MOE_TM), lambda s, f, e, t0, nt: (jnp.minimum(t0[s] + i, n_tiles - 1), 0, 0),
                            memory_space=pltpu.SMEM)

    grid_spec = pltpu.PrefetchScalarGridSpec(
        num_scalar_prefetch=3,
        grid=(n_sb, MOE_NF),
        in_specs=[dst_spec(i) for i in range(MOE_SB_TILES)] + [
            pl.BlockSpec(memory_space=pl.ANY),
            pl.BlockSpec((None, None, D_MODEL, MOE_TF), lambda s, f, e, t0, nt: (layer, e[s], 0, ff(s, f, nt))),
            pl.BlockSpec((None, None, D_MODEL, MOE_TF),
                         lambda s, f, e, t0, nt: (layer, e[s], 0, MOE_NF + ff(s, f, nt))),
            pl.BlockSpec((None, None, 1, MOE_TF), lambda s, f, e, t0, nt: (layer, e[s], 0, ff(s, f, nt))),
            pl.BlockSpec((None, None, 1, MOE_TF), lambda s, f, e, t0, nt: (layer, e[s], 0, MOE_NF + ff(s, f, nt))),
            pl.BlockSpec((None, None, MOE_TF, D_MODEL), lambda s, f, e, t0, nt: (layer, e[s], ff(s, f, nt), 0)),
            pl.BlockSpec((None, None, 1, D_MODEL), lambda s, f, e, t0, nt: (layer, e[s], 0, 0)),
        ],
        out_specs=pl.BlockSpec(memory_space=pl.ANY),
        scratch_shapes=[
            pltpu.VMEM((MOE_SB_ROWS, D_MODEL), BF16),
            pltpu.VMEM((MOE_SB_ROWS, D_MODEL), F32),
            pltpu.VMEM((D_MODEL, 2 * MOE_TF), BF16),
            pltpu.VMEM((MOE_TF, D_MODEL), BF16),
            pltpu.SemaphoreType.DMA(()),
            pltpu.SemaphoreType.DMA(()),
        ],
    )
    n_l = w_up.shape[0]
    b_up4 = b_up.reshape(n_l, N_EXPERTS, 1, 2 * D_FF)
    return pl.pallas_call(
        _moe_expert_kernel,
        grid_spec=grid_spec,
        out_shape=jax.ShapeDtypeStruct((n_pad * TOP_K, D_MODEL), F32),
        compiler_params=_cparams("arbitrary", "arbitrary"),
        name="moe_experts",
    )(sb_e, sb_tile0, sb_ntile, *([slot_dst] * MOE_SB_TILES), h,
      w_up, w_up, b_up4, b_up4, w_down, b_down.reshape(n_l, N_EXPERTS, 1, D_MODEL))


def _moe_combine_kernel(n_ptiles, n_s, xp_ref, xs_ref, y_ref, g_ref, gp_ref, gs_ref, op_ref, os_ref):
    t = pl.program_id(0)

    def mix(rows):
        acc = None
        for k in range(TOP_K):
            term = g_ref[0:rows, k:k + 1] * y_ref[0:rows, k * D_MODEL:(k + 1) * D_MODEL]
            acc = term if acc is None else acc + term
        return acc

    @pl.when(t < n_ptiles)
    def _():
        op_ref[...] = xp_ref[...] + gp_ref[...] * mix(ROW_TILE)

    @pl.when(t == n_ptiles)
    def _():
        os_ref[...] = xs_ref[...] + gs_ref[...] * mix(n_s)


def _moe_combine(xp, xs, y, gates_t, mods):
    n_p, n_s = xp.shape[0], xs.shape[0]
    n_pt = n_p // ROW_TILE
    seq_tiles = _seq_rows(xp) // ROW_TILE
    pt = lambda t: jnp.minimum(t, n_pt - 1)
    return pl.pallas_call(
        functools.partial(_moe_combine_kernel, n_pt, n_s),
        grid=(n_pt + 1,),
        in_specs=[pl.BlockSpec((ROW_TILE, D_MODEL), lambda t: (pt(t), 0)),
                  pl.BlockSpec((n_s, D_MODEL), lambda t: (0, 0)),
                  pl.BlockSpec((ROW_TILE, TOP_K * D_MODEL), lambda t: (t, 0)),
                  pl.BlockSpec((ROW_TILE, TOP_K), lambda t: (t, 0)),
                  mods.prompt_spec(5, D_MODEL, lambda t: pt(t) // seq_tiles),
                  mods.sample_spec(5, D_MODEL, n_s)],
        out_specs=[pl.BlockSpec((ROW_TILE, D_MODEL), lambda t: (pt(t), 0)),
                   pl.BlockSpec((n_s, D_MODEL), lambda t: (0, 0))],
        out_shape=[jax.ShapeDtypeStruct((n_p, D_MODEL), F32), jax.ShapeDtypeStruct((n_s, D_MODEL), F32)],
        compiler_params=_cparams("arbitrary"),
        name="moe_combine",
    )(xp, xs, y.reshape(-1, TOP_K * D_MODEL), gates_t, mods.m4, mods.m3)


def _moe_layer(xp, xs, gain, mods, layer, w_router, b_router, w_up, b_up, w_down, b_down):
    n_tok = xp.shape[0] + xs.shape[0]
    h, idx, gates, rank, cnt = _router(xp, xs, gain, mods, w_router[layer], b_router[layer])
    counts = cnt[:, 0].astype(I32)
    y = _moe_experts(h, idx, rank, counts, layer, w_up, b_up, w_down, b_down, n_tok)
    return _moe_combine(xp, xs, y, gates.T, mods)


def _qk_columns(qk_gain, n_groups):
    nq, nk = n_groups * N_HEADS, n_groups * KV_HEADS
    gain = jnp.concatenate([jnp.tile(qk_gain[0] * HEAD_DIM ** -0.5, nq), jnp.tile(qk_gain[1], nk),
                            jnp.ones((nk * HEAD_DIM,), F32)])
    flag = jnp.concatenate([jnp.ones(((nq + nk) * HEAD_DIM,), F32), jnp.zeros((nk * HEAD_DIM,), F32)])
    return gain[None, :], flag[None, :]


def _to_residue_major(x, batch):
    seq = x.shape[0] // batch
    return x.reshape(batch, seq // RES, RES, -1).transpose(0, 2, 1, 3).reshape(batch * seq, -1)


def _prompt_state(qkv_p, batch, k_off, v_off, win):
    seq = qkv_p.shape[0] // batch
    w = min(win, seq)
    r = qkv_p.reshape(batch, RES, seq // RES, -1)[:, :, (seq - w) // RES:]

    def take(off):
        a = r[..., off:off + KV_WIDTH].transpose(0, 2, 1, 3)
        return a.reshape(batch, w, KV_HEADS, HEAD_DIM)
    return jnp.stack([take(k_off), take(v_off)], axis=1)


def kernel(x_prompt, x_sample, c_prompt, c_sample, cache_a_kv, cache_b1_kv, cache_b2_kv, cache_b3_kv,
           rel_bias, norm_gain, w_ada, b_ada, w_qkv_a, qk_gain_a, sinks_a, w_o_a,
           w_qkv_b, qk_gain_b, w_o_b, w_router, b_router, w_up, b_up, w_down, b_down):
    b, s, d = x_prompt.shape
    bd, t, _ = x_sample.shape
    xp = _to_residue_major(x_prompt.reshape(b * s, d), b)
    xs = x_sample.reshape(bd * t, d)
    c_all = jnp.concatenate([c_sample, c_prompt, jnp.zeros((N_MOD_ROWS - bd - b, d), F32)], axis=0)
    mods_all = _ada(c_all, w_ada, b_ada)
    bias_p = [_prompt_bias(rel_bias, dil) for _, dil in B_PATTERNS]
    bias_d = [_decode_bias(rel_bias, dil) for _, dil in B_PATTERNS]
    caches_b = (cache_b1_kv, cache_b2_kv, cache_b3_kv)
    a_p, a_s = [], []
    b_p = [[], [], []]
    b_s = [[], [], []]
    for i in range(DEPTH):
        mods = _Mods(mods_all, i)
        j = i // 2
        hp, hs = _normmod_rows(xp, xs, norm_gain[i, 0][None, :], mods, 0, 1)
        if i % 2 == 0:
            gain, flag = _qk_columns(qk_gain_a[j], 1)
            qkv_p, qkv_s = _mm_qkv(hp, hs, w_qkv_a, j, gain, flag, 1280)
            op = _prompt_attention(qkv_p, QKV_A, 0, ATTN_WIDTH, ATTN_WIDTH + KV_WIDTH, 1, bias_p[0],
                                   sinks_a[j], F32, False)
            os_, st_s = _decode(qkv_s, [cache_a_kv], j, ((WINDOW_A, 1),), bias_d[0][None], sinks_a[j])
            a_p.append(_prompt_state(qkv_p, b, ATTN_WIDTH, ATTN_WIDTH + KV_WIDTH, WINDOW_A))
            a_s.append(st_s[0])
            w_o = w_o_a
        else:
            gain, flag = _qk_columns(qk_gain_b[j], N_GROUPS_B)
            qkv_p, qkv_s = _mm_qkv(hp, hs, w_qkv_b, j, gain, flag, 1536)
            k0 = N_GROUPS_B * ATTN_WIDTH
            v0 = k0 + N_GROUPS_B * KV_WIDTH
            outs, lses = [], []
            for gi, (win, dil) in enumerate(B_PATTERNS):
                o_g, l_g = _prompt_attention(qkv_p, QKV_B, gi * ATTN_WIDTH, k0 + gi * KV_WIDTH,
                                             v0 + gi * KV_WIDTH, dil, bias_p[gi], None, F32, True)
                outs.append(o_g)
                lses.append(l_g)
                b_p[gi].append(_prompt_state(qkv_p, b, k0 + gi * KV_WIDTH, v0 + gi * KV_WIDTH, win))
            op = _combine_groups(outs, lses)
            os_, st_s = _decode(qkv_s, caches_b, j, B_PATTERNS, jnp.stack(bias_d), None)
            for gi in range(N_GROUPS_B):
                b_s[gi].append(st_s[gi])
            w_o = w_o_b
        xp, xs = _mm_resid(op, os_, w_o, j, xp, xs, mods, 2, 1024)
        xp, xs = _moe_layer(xp, xs, norm_gain[i, 1][None, :], mods, i, w_router, b_router,
                            w_up, b_up, w_down, b_down)
    xp = xp.reshape(b, RES, s // RES, d).transpose(0, 2, 1, 3)
    return (xp.reshape(b, s, d), xs.reshape(bd, t, d),
            jnp.stack(a_p), jnp.stack(a_s),
            jnp.stack(b_p[0]), jnp.stack(b_s[0]),
            jnp.stack(b_p[1]), jnp.stack(b_s[1]),
            jnp.stack(b_p[2]), jnp.stack(b_s[2]))
```

```python
import functools
import math

import jax
import jax.numpy as jnp
import numpy as np
from jax import lax
from jax.experimental import pallas as pl
from jax.experimental.pallas import tpu as pltpu

D_MODEL = 2048
DEPTH = 4
HEAD_DIM = 64
N_HEADS = 32
KV_HEADS = 4
HEADS_PER_KV = N_HEADS // KV_HEADS
KV_WIDTH = KV_HEADS * HEAD_DIM
WINDOW_A = 128
B_PATTERNS = ((128, 1), (512, 4), (2048, 16))
N_GROUPS_B = 3
RES = 16
N_BUCKETS = 32
MAX_DISTANCE = 2048
N_EXPERTS = 32
TOP_K = 4
D_FF = 2048
SWIGLU_ALPHA = 1.702
SWIGLU_LIMIT = 7.0
ATTN_BLOCK = 128
NORM_EPS = 1e-6
NEG_INF = -1e30
ATTN_WIDTH = N_HEADS * HEAD_DIM
QKV_A = ATTN_WIDTH + 2 * KV_WIDTH
QKV_B = N_GROUPS_B * QKV_A

F32 = jnp.float32
BF16 = jnp.bfloat16
I32 = jnp.int32

VMEM_LIMIT_BYTES = 56 * 1024 * 1024
MXU_DIM = 256

N_MOD_ROWS = 40
MOD_PROMPT_ROW0 = 32
ROW_TILE = 256
MM_BM = 512
ADA_BN = 1024

MOE_TM = 256
MOE_SB_TILES = 5
MOE_SB_ROWS = MOE_TM * MOE_SB_TILES
MOE_TF = 512
MOE_NF = D_FF // MOE_TF
MOE_ROW_UNROLL = 8


def _cparams(*sem):
    return pltpu.CompilerParams(dimension_semantics=sem, vmem_limit_bytes=VMEM_LIMIT_BYTES)


def _dot(a, b):
    return jnp.dot(a, b, preferred_element_type=F32)


def _dot_nt(a, b):
    return lax.dot_general(a, b, (((1,), (1,)), ((), ())), preferred_element_type=F32)


def _ada_kernel(c_ref, w_ref, b_ref, o_ref):
    c = c_ref[...]
    a = (c / (1.0 + jnp.exp(-c))).astype(BF16)
    o_ref[...] = _dot(a, w_ref[...].astype(BF16)) + b_ref[...]


def _ada(c_all, w_ada, b_ada):
    n6 = w_ada.shape[-1]
    return pl.pallas_call(
        _ada_kernel,
        grid=(DEPTH, n6 // ADA_BN),
        in_specs=[pl.BlockSpec((N_MOD_ROWS, D_MODEL), lambda i, n: (0, 0)),
                  pl.BlockSpec((None, D_MODEL, ADA_BN), lambda i, n: (i, 0, n)),
                  pl.BlockSpec((None, 1, ADA_BN), lambda i, n: (i, 0, n))],
        out_specs=pl.BlockSpec((None, N_MOD_ROWS, ADA_BN), lambda i, n: (i, 0, n)),
        out_shape=jax.ShapeDtypeStruct((DEPTH, N_MOD_ROWS, n6), F32),
        compiler_params=_cparams("arbitrary", "arbitrary"),
        name="ada",
    )(c_all, w_ada, b_ada.reshape(DEPTH, 1, n6))


class _Mods:
    def __init__(self, mods, layer):
        self.m3 = mods
        self.m4 = mods.reshape(DEPTH, N_MOD_ROWS, 1, mods.shape[-1])
        self.layer = layer

    def prompt_spec(self, chunk, width, seq_of_step, col_of_step=None):
        i, per = self.layer, D_MODEL // width

        def imap(*g):
            col = 0 if col_of_step is None else col_of_step(*g)
            return (i, MOD_PROMPT_ROW0 + seq_of_step(*g), 0, chunk * per + col)
        return pl.BlockSpec((None, None, 1, width), imap)

    def sample_spec(self, chunk, width, n_rows, col_of_step=None):
        i, per = self.layer, D_MODEL // width

        def imap(*g):
            col = 0 if col_of_step is None else col_of_step(*g)
            return (i, 0, chunk * per + col)
        return pl.BlockSpec((None, n_rows, width), imap)


def _normmod(x, gain, shift, scale):
    y = x * lax.rsqrt(jnp.mean(x * x, axis=-1, keepdims=True) + NORM_EPS)
    return (y * gain) * (1.0 + scale) + shift


def _normmod_kernel(x_ref, g_ref, sh_ref, sc_ref, o_ref):
    o_ref[...] = _normmod(x_ref[...], g_ref[...], sh_ref[...], sc_ref[...]).astype(o_ref.dtype)


def _normmod_rows(xp, xs, gain, mods, c_shift, c_scale):
    n_p, n_s = xp.shape[0], xs.shape[0]
    seq_tiles = _seq_rows(xp) // ROW_TILE
    gspec = pl.BlockSpec((1, D_MODEL), lambda t: (0, 0))
    hp = pl.pallas_call(
        _normmod_kernel,
        grid=(n_p // ROW_TILE,),
        in_specs=[pl.BlockSpec((ROW_TILE, D_MODEL), lambda t: (t, 0)), gspec,
                  mods.prompt_spec(c_shift, D_MODEL, lambda t: t // seq_tiles),
                  mods.prompt_spec(c_scale, D_MODEL, lambda t: t // seq_tiles)],
        out_specs=pl.BlockSpec((ROW_TILE, D_MODEL), lambda t: (t, 0)),
        out_shape=jax.ShapeDtypeStruct((n_p, D_MODEL), BF16),
        compiler_params=_cparams("arbitrary"),
        name="normmod_prompt",
    )(xp, gain, mods.m4, mods.m4)
    hs = pl.pallas_call(
        _normmod_kernel,
        grid=(1,),
        in_specs=[pl.BlockSpec((n_s, D_MODEL), lambda t: (0, 0)), gspec,
                  mods.sample_spec(c_shift, D_MODEL, n_s), mods.sample_spec(c_scale, D_MODEL, n_s)],
        out_specs=pl.BlockSpec((n_s, D_MODEL), lambda t: (0, 0)),
        out_shape=jax.ShapeDtypeStruct((n_s, D_MODEL), BF16),
        compiler_params=_cparams("arbitrary"),
        name="normmod_sample",
    )(xs, gain, mods.m3, mods.m3)
    return hp, hs


_SEQ_ROWS = 2048


def _seq_rows(xp):
    return _SEQ_ROWS


def _head_sumsq(y):
    y2 = y * y
    hi = y2.astype(BF16)
    lo = (y2 - hi.astype(F32)).astype(BF16)
    r = lax.broadcasted_iota(I32, (MXU_DIM, MXU_DIM), 0) // HEAD_DIM
    c = lax.broadcasted_iota(I32, (MXU_DIM, MXU_DIM), 1) // HEAD_DIM
    seg = jnp.where(r == c, 1.0, 0.0).astype(BF16)
    outs = []
    for j in range(y.shape[1] // MXU_DIM):
        sl = slice(j * MXU_DIM, (j + 1) * MXU_DIM)
        outs.append(_dot(hi[:, sl], seg) + _dot(lo[:, sl], seg))
    return jnp.concatenate(outs, axis=1)


def _qkv_epilogue(acc, gain, flag):
    inv = lax.rsqrt(_head_sumsq(acc) * (1.0 / HEAD_DIM) + NORM_EPS)
    return jnp.where(flag > 0.5, (acc * inv) * gain, acc)


def _mm_qkv_kernel(n_ptiles, ap_ref, as_ref, w_ref, gain_ref, flag_ref, op_ref, os_ref, wb_ref):
    t = pl.program_id(1)

    @pl.when(t == 0)
    def _():
        wb_ref[...] = w_ref[...].astype(BF16)

    @pl.when(t < n_ptiles)
    def _():
        op_ref[...] = _qkv_epilogue(_dot(ap_ref[...], wb_ref[...]), gain_ref[...], flag_ref[...])

    @pl.when(t == n_ptiles)
    def _():
        os_ref[...] = _qkv_epilogue(_dot(as_ref[...], wb_ref[...]), gain_ref[...], flag_ref[...])


def _mm_qkv(hp, hs, w, layer, col_gain, col_flag, bn):
    n_p, n_s, n = hp.shape[0], hs.shape[0], w.shape[2]
    n_pt = n_p // MM_BM
    pt = lambda c, t: (jnp.minimum(t, n_pt - 1), 0)
    return pl.pallas_call(
        functools.partial(_mm_qkv_kernel, n_pt),
        grid=(n // bn, n_pt + 1),
        in_specs=[pl.BlockSpec((MM_BM, D_MODEL), pt),
                  pl.BlockSpec((n_s, D_MODEL), lambda c, t: (0, 0)),
                  pl.BlockSpec((None, D_MODEL, bn), lambda c, t: (layer, 0, c)),
                  pl.BlockSpec((1, bn), lambda c, t: (0, c)),
                  pl.BlockSpec((1, bn), lambda c, t: (0, c))],
        out_specs=[pl.BlockSpec((MM_BM, bn), lambda c, t: (jnp.minimum(t, n_pt - 1), c)),
                   pl.BlockSpec((n_s, bn), lambda c, t: (0, c))],
        out_shape=[jax.ShapeDtypeStruct((n_p, n), F32), jax.ShapeDtypeStruct((n_s, n), F32)],
        scratch_shapes=[pltpu.VMEM((D_MODEL, bn), BF16)],
        compiler_params=_cparams("arbitrary", "arbitrary"),
        name="mm_qkv",
    )(hp, hs, w, col_gain, col_flag)


def _mm_resid_kernel(n_ptiles, ap_ref, as_ref, w_ref, xp_ref, xs_ref, gp_ref, gs_ref, op_ref, os_ref, wb_ref):
    t = pl.program_id(1)

    @pl.when(t == 0)
    def _():
        wb_ref[...] = w_ref[...].astype(BF16)

    @pl.when(t < n_ptiles)
    def _():
        op_ref[...] = xp_ref[...] + gp_ref[...] * _dot(ap_ref[...].astype(BF16), wb_ref[...])

    @pl.when(t == n_ptiles)
    def _():
        os_ref[...] = xs_ref[...] + gs_ref[...] * _dot(as_ref[...], wb_ref[...])


def _mm_resid(ap, as_, w, layer, xp, xs, mods, c_gate, bn):
    n_p, n_s, k, n = ap.shape[0], as_.shape[0], w.shape[1], w.shape[2]
    n_pt = n_p // MM_BM
    seq_tiles = _seq_rows(xp) // MM_BM
    prow = lambda c, t: jnp.minimum(t, n_pt - 1)
    return pl.pallas_call(
        functools.partial(_mm_resid_kernel, n_pt),
        grid=(n // bn, n_pt + 1),
        in_specs=[pl.BlockSpec((MM_BM, k), lambda c, t: (prow(c, t), 0)),
                  pl.BlockSpec((n_s, k), lambda c, t: (0, 0)),
                  pl.BlockSpec((None, k, bn), lambda c, t: (layer, 0, c)),
                  pl.BlockSpec((MM_BM, bn), lambda c, t: (prow(c, t), c)),
                  pl.BlockSpec((n_s, bn), lambda c, t: (0, c)),
                  mods.prompt_spec(c_gate, bn, lambda c, t: prow(c, t) // seq_tiles, lambda c, t: c),
                  mods.sample_spec(c_gate, bn, n_s, lambda c, t: c)],
        out_specs=[pl.BlockSpec((MM_BM, bn), lambda c, t: (prow(c, t), c)),
                   pl.BlockSpec((n_s, bn), lambda c, t: (0, c))],
        out_shape=[jax.ShapeDtypeStruct((n_p, n), F32), jax.ShapeDtypeStruct((n_s, n), F32)],
        scratch_shapes=[pltpu.VMEM((k, bn), BF16)],
        compiler_params=_cparams("arbitrary", "arbitrary"),
        name="mm_resid",
    )(ap, as_, w, xp, xs, mods.m4, mods.m3)


def _attn_kernel(has_sink, with_lse, *refs):
    if has_sink:
        sink_ref, refs = refs[0], refs[1:]
    q_refs = refs[:KV_HEADS]
    kc_ref, kp_ref, vc_ref, vp_ref, bias_ref = refs[KV_HEADS:KV_HEADS + 5]
    o_ref = refs[KV_HEADS + 5]
    lse_ref = refs[KV_HEADS + 6] if with_lse else None

    def rows(ref):
        return ref[...].reshape(ATTN_BLOCK, ref.shape[-1])

    kw = jnp.concatenate([rows(kp_ref), rows(kc_ref)], axis=0).astype(BF16)
    vw = jnp.concatenate([rows(vp_ref), rows(vc_ref)], axis=0).astype(BF16)
    lane = lax.broadcasted_iota(I32, (ATTN_BLOCK, 128), 1)
    lse_all = jnp.zeros((ATTN_BLOCK, 128), F32)
    for kh in range(KV_HEADS):
        k_h = kw[:, kh * HEAD_DIM:(kh + 1) * HEAD_DIM]
        q_all = rows(q_refs[kh]).astype(BF16)
        outs = []
        for g in range(HEADS_PER_KV):
            h = kh * HEADS_PER_KV + g
            s = _dot_nt(q_all[:, g * HEAD_DIM:(g + 1) * HEAD_DIM], k_h) + bias_ref[h]
            m = jnp.max(s, axis=-1, keepdims=True)
            if has_sink:
                m = jnp.maximum(m, sink_ref[h])
            p = jnp.exp(s - m)
            den = jnp.sum(p, axis=-1, keepdims=True)
            if has_sink:
                den = den + jnp.exp(sink_ref[h] - m)
            o_full = _dot(p.astype(BF16), vw)
            outs.append(o_full[:, kh * HEAD_DIM:(kh + 1) * HEAD_DIM] / den)
            if with_lse:
                lse_all = jnp.where(lane == h, m + jnp.log(den), lse_all)
        lo, w = kh * HEADS_PER_KV * HEAD_DIM, HEADS_PER_KV * HEAD_DIM
        o_kh = jnp.concatenate(outs, axis=1).astype(o_ref.dtype)
        o_ref[..., lo:lo + w] = o_kh.reshape(o_ref.shape[:-1] + (w,))
    if with_lse:
        lse_ref[...] = lse_all.reshape(lse_ref.shape)


def _prompt_attention(qkv_p, width, q_off, k_off, v_off, dil, bias, sinks, out_dtype, with_lse):
    n_p = qkv_p.shape[0]
    batch = n_p // _seq_rows(qkv_p)
    m_per = _seq_rows(qkv_p) // RES
    rr = RES // 4
    nb = RES // dil
    qw = HEADS_PER_KV * HEAD_DIM
    has_sink = sinks is not None

    def view(a):
        return a.reshape(batch, rr, 4, m_per, a.shape[-1])

    def spec(cols, col_block, prev):
        def blk(j):
            return jnp.maximum(j - 1, 0) if prev else j
        if dil == 1:
            return pl.BlockSpec((None, rr, 4, ATTN_BLOCK // RES, cols), lambda b, j, r: (b, 0, 0, blk(j), col_block))
        if dil == 4:
            return pl.BlockSpec((None, rr, None, ATTN_BLOCK // rr, cols), lambda b, j, r: (b, 0, r, blk(j), col_block))
        return pl.BlockSpec((None, None, None, ATTN_BLOCK, cols), lambda b, j, r: (b, r // 4, r % 4, 0, col_block))

    in_specs = [spec(qw, q_off // qw + kh, False) for kh in range(KV_HEADS)]
    in_specs += [spec(KV_WIDTH, k_off // KV_WIDTH, False), spec(KV_WIDTH, k_off // KV_WIDTH, True),
                 spec(KV_WIDTH, v_off // KV_WIDTH, False), spec(KV_WIDTH, v_off // KV_WIDTH, True),
                 pl.BlockSpec((None, N_HEADS, ATTN_BLOCK, 2 * ATTN_BLOCK),
                              lambda b, j, r: (jnp.where(j == 0, 1, 0), 0, 0, 0))]
    args = [view(qkv_p)] * (KV_HEADS + 4) + [bias]
    if has_sink:
        in_specs = [pl.BlockSpec(memory_space=pltpu.SMEM)] + in_specs
        args = [sinks] + args
    out_specs = [spec(ATTN_WIDTH, 0, False)]
    out_shape = [jax.ShapeDtypeStruct((batch, rr, 4, m_per, ATTN_WIDTH), out_dtype)]
    if with_lse:
        out_specs.append(spec(128, 0, False))
        out_shape.append(jax.ShapeDtypeStruct((batch, rr, 4, m_per, 128), F32))
    res = pl.pallas_call(
        functools.partial(_attn_kernel, has_sink, with_lse),
        grid=(batch, nb, dil),
        in_specs=in_specs, out_specs=out_specs, out_shape=out_shape,
        compiler_params=_cparams("arbitrary", "arbitrary", "arbitrary"),
        name="prompt_attention",
    )(*args)
    o = res[0].reshape(n_p, ATTN_WIDTH)
    if with_lse:
        return o, res[1].reshape(n_p, 128)
    return o


def _block_offsets(dil):
    i = np.arange(ATTN_BLOCK)
    if dil == 1:
        return (i % (ATTN_BLOCK // RES)) * RES + i // (ATTN_BLOCK // RES)
    if dil == 4:
        per = ATTN_BLOCK // (RES // 4)
        return (i % per) * (RES // 4) + i // per
    return i


def _t5_bucket_static(dist):
    dist = np.maximum(np.asarray(dist, np.int64), 0)
    max_exact = N_BUCKETS // 2
    d = dist.astype(np.float32)
    ratio = np.log(np.maximum(d, np.float32(1.0)) / np.float32(max_exact)) / np.float32(
        math.log(MAX_DISTANCE / max_exact))
    large = max_exact + (ratio.astype(np.float32) * np.float32(N_BUCKETS - max_exact)).astype(np.int32)
    large = np.minimum(large, N_BUCKETS - 1)
    return np.where(dist < max_exact, dist, large).astype(np.int32)


def _bias_lookup(rel_bias, bucket):
    onehot = jnp.asarray(bucket[None] == np.arange(N_BUCKETS).reshape((N_BUCKETS,) + (1,) * bucket.ndim))
    tab = rel_bias.astype(F32).reshape((N_BUCKETS, N_HEADS) + (1,) * bucket.ndim)
    return jnp.sum(jnp.where(onehot[:, None], tab, 0.0), axis=0)


def _prompt_bias(rel_bias, dil):
    off = _block_offsets(dil)
    q_pos = ATTN_BLOCK + off[:, None]
    k_pos = np.concatenate([off, ATTN_BLOCK + off])[None, :]
    dist = q_pos - k_pos
    bias = _bias_lookup(rel_bias, _t5_bucket_static(dist * dil))
    valid = (dist >= 0) & (dist <= ATTN_BLOCK)
    first = valid & (k_pos >= ATTN_BLOCK)
    return jnp.stack([jnp.where(jnp.asarray(valid)[None], bias, NEG_INF),
                      jnp.where(jnp.asarray(first)[None], bias, NEG_INF)])


def _decode_bias(rel_bias, dil):
    steps = np.concatenate([ATTN_BLOCK - np.arange(ATTN_BLOCK), [0], np.zeros(ATTN_BLOCK - 1, np.int64)])
    b = _bias_lookup(rel_bias, _t5_bucket_static(dil * steps))
    return jnp.where(jnp.asarray(np.arange(2 * ATTN_BLOCK) <= ATTN_BLOCK)[None], b, 0.0)


def _bcast_heads(w):
    r = lax.broadcasted_iota(I32, (128, ATTN_WIDTH), 0)
    c = lax.broadcasted_iota(I32, (128, ATTN_WIDTH), 1) // HEAD_DIM
    e = jnp.where(r == c, 1.0, 0.0).astype(BF16)
    hi = w.astype(BF16)
    r1 = w - hi.astype(F32)
    mid = r1.astype(BF16)
    lo = (r1 - mid.astype(F32)).astype(BF16)
    return _dot(hi, e) + _dot(mid, e) + _dot(lo, e)


def _combine_kernel(o1_ref, o2_ref, o3_ref, l1_ref, l2_ref, l3_ref, o_ref):
    l1, l2, l3 = l1_ref[...], l2_ref[...], l3_ref[...]
    m = jnp.maximum(jnp.maximum(l1, l2), l3)
    e1, e2, e3 = jnp.exp(l1 - m), jnp.exp(l2 - m), jnp.exp(l3 - m)
    inv = 1.0 / (e1 + e2 + e3)
    acc = _bcast_heads(e1 * inv) * o1_ref[...]
    acc = acc + _bcast_heads(e2 * inv) * o2_ref[...]
    acc = acc + _bcast_heads(e3 * inv) * o3_ref[...]
    o_ref[...] = acc.astype(o_ref.dtype)


def _combine_groups(outs, lses):
    n_p = outs[0].shape[0]
    ospec = pl.BlockSpec((ROW_TILE, ATTN_WIDTH), lambda t: (t, 0))
    lspec = pl.BlockSpec((ROW_TILE, 128), lambda t: (t, 0))
    return pl.pallas_call(
        _combine_kernel,
        grid=(n_p // ROW_TILE,),
        in_specs=[ospec] * 3 + [lspec] * 3,
        out_specs=ospec,
        out_shape=jax.ShapeDtypeStruct((n_p, ATTN_WIDTH), BF16),
        compiler_params=_cparams("arbitrary"),
        name="combine_groups",
    )(*outs, *lses)


def _decode_kernel(groups, has_sink, *refs):
    ng = len(groups)
    q_ref, kv_ref, bias_ref = refs[:3]
    pos = 3
    if has_sink:
        sink_ref = refs[pos]
        pos += 1
    cache_refs = refs[pos:pos + ng]
    state_refs = refs[pos + ng:pos + 2 * ng]
    o_ref = refs[pos + 2 * ng]

    row_kh = lax.broadcasted_iota(I32, (N_HEADS, HEAD_DIM), 0) // HEADS_PER_KV
    outs, lses = [], []
    for g, (win, dil) in enumerate(groups):
        c_ref, st_ref = cache_refs[g], state_refs[g]
        k_new = kv_ref[:, g * KV_WIDTH:(g + 1) * KV_WIDTH]
        v_new = kv_ref[:, (ng + g) * KV_WIDTH:(ng + g + 1) * KV_WIDTH]
        for kv, new in ((0, k_new), (1, v_new)):
            st_ref[kv, pl.ds(0, 2 * win - 2), :] = c_ref[kv, pl.ds(2, 2 * win - 2), :]
            st_ref[kv, pl.ds(2 * win - 2, 1), :] = new[:, :128]
            st_ref[kv, pl.ds(2 * win - 1, 1), :] = new[:, 128:]
        k_sel = jnp.concatenate([c_ref[0, pl.ds(half, ATTN_BLOCK, stride=2 * dil), :] for half in (0, 1)], axis=1)
        v_sel = jnp.concatenate([c_ref[1, pl.ds(half, ATTN_BLOCK, stride=2 * dil), :] for half in (0, 1)], axis=1)
        q = q_ref[g]
        q_exp = jnp.concatenate([jnp.where(row_kh == kh, q, 0.0) for kh in range(KV_HEADS)], axis=1)
        qb = q_exp.astype(BF16)
        bias = bias_ref[g]
        s = _dot_nt(qb, k_sel.astype(BF16)) + bias[:, :ATTN_BLOCK]
        s0 = jnp.sum(qb.astype(F32) * k_new.astype(BF16).astype(F32), axis=-1, keepdims=True)
        s0 = s0 + bias[:, ATTN_BLOCK:ATTN_BLOCK + 1]
        m = jnp.maximum(jnp.max(s, axis=-1, keepdims=True), s0)
        if has_sink:
            m = jnp.maximum(m, sink_ref[...])
        p = jnp.exp(s - m)
        p0 = jnp.exp(s0 - m)
        den = jnp.sum(p, axis=-1, keepdims=True) + p0
        if has_sink:
            den = den + jnp.exp(sink_ref[...] - m)
        o_full = _dot(p.astype(BF16), v_sel.astype(BF16))
        o_full = o_full + p0.astype(BF16).astype(F32) * v_new.astype(BF16).astype(F32)
        o_h = jnp.zeros((N_HEADS, HEAD_DIM), F32)
        for kh in range(KV_HEADS):
            o_h = o_h + jnp.where(row_kh == kh, o_full[:, kh * HEAD_DIM:(kh + 1) * HEAD_DIM], 0.0)
        outs.append(o_h / den)
        lses.append(m + jnp.log(den))
    if ng == 1:
        o = outs[0]
    else:
        m = functools.reduce(jnp.maximum, lses)
        es = [jnp.exp(l - m) for l in lses]
        tot = functools.reduce(lambda a, b: a + b, es)
        o = functools.reduce(lambda a, b: a + b, [(e / tot) * og for e, og in zip(es, outs)])
    o_ref[...] = o.astype(o_ref.dtype)


def _decode(qkv_s, caches, layer, groups, biases, sinks):
    n_s = qkv_s.shape[0]
    ng = len(groups)
    q = qkv_s[:, :ng * ATTN_WIDTH].reshape(n_s, ng, N_HEADS, HEAD_DIM)
    kv = qkv_s[:, ng * ATTN_WIDTH:].reshape(n_s, 1, 2 * ng * KV_WIDTH)
    has_sink = sinks is not None
    in_specs = [pl.BlockSpec((None, ng, N_HEADS, HEAD_DIM), lambda b: (b, 0, 0, 0)),
                pl.BlockSpec((None, 1, 2 * ng * KV_WIDTH), lambda b: (b, 0, 0)),
                pl.BlockSpec((ng, N_HEADS, 2 * ATTN_BLOCK), lambda b: (0, 0, 0))]
    args = [q, kv, biases]
    if has_sink:
        in_specs.append(pl.BlockSpec((N_HEADS, 1), lambda b: (0, 0)))
        args.append(sinks.reshape(N_HEADS, 1))
    out_specs, out_shape = [], []
    for (win, _), c in zip(groups, caches):
        in_specs.append(pl.BlockSpec((None, None, 2, 2 * win, 128), lambda b: (layer, b, 0, 0, 0)))
        args.append(c.reshape(c.shape[0], n_s, 2, 2 * win, 128))
        out_specs.append(pl.BlockSpec((None, 2, 2 * win, 128), lambda b: (b, 0, 0, 0)))
        out_shape.append(jax.ShapeDtypeStruct((n_s, 2, 2 * win, 128), F32))
    out_specs.append(pl.BlockSpec((None, N_HEADS, HEAD_DIM), lambda b: (b, 0, 0)))
    out_shape.append(jax.ShapeDtypeStruct((n_s, N_HEADS, HEAD_DIM), BF16))
    res = pl.pallas_call(
        functools.partial(_decode_kernel, tuple(groups), has_sink),
        grid=(n_s,),
        in_specs=in_specs, out_specs=out_specs, out_shape=out_shape,
        compiler_params=_cparams("arbitrary"),
        name="decode_attention",
    )(*args)
    states = [r.reshape(n_s, 2, win, KV_HEADS, HEAD_DIM) for r, (win, _) in zip(res[:ng], groups)]
    return res[ng].reshape(n_s, ATTN_WIDTH), states


def _route(h, n_valid, wr_ref, br_ref, idx_ref, gate_ref, rank_ref, cnt_ref):
    tm = ROW_TILE
    logits = _dot_nt(wr_ref[...], h.astype(BF16)) + br_ref[...]
    e_iota = lax.broadcasted_iota(I32, (N_EXPERTS, tm), 0)
    valid = lax.broadcasted_iota(I32, (N_EXPERTS, tm), 1) < n_valid
    l = logits
    vals, sels = [], []
    for k in range(TOP_K):
        m = jnp.max(l, axis=0, keepdims=True)
        idx = jnp.min(jnp.where(l == m, e_iota, N_EXPERTS), axis=0, keepdims=True)
        sel = e_iota == idx
        idx_ref[pl.ds(k, 1), :] = idx
        vals.append(m)
        sels.append(sel & valid)
        l = jnp.where(sel, -jnp.inf, l)
    es = [jnp.exp(v - vals[0]) for v in vals]
    tot = es[0] + es[1] + es[2] + es[3]
    for k in range(TOP_K):
        gate_ref[pl.ds(k, 1), :] = es[k] / tot
    oh = jnp.concatenate([jnp.where(s, 1.0, 0.0) for s in sels], axis=0)
    r = lax.broadcasted_iota(I32, (tm, tm), 0)
    c = lax.broadcasted_iota(I32, (tm, tm), 1)
    upper = jnp.where(r < c, 1.0, 0.0).astype(BF16)
    prefix = _dot(oh.astype(BF16), upper)
    base = cnt_ref[:, 0:1]
    for k in range(TOP_K):
        oh_k = oh[k * N_EXPERTS:(k + 1) * N_EXPERTS]
        rank = jnp.sum(oh_k * (base + prefix[k * N_EXPERTS:(k + 1) * N_EXPERTS]), axis=0, keepdims=True)
        rank_ref[pl.ds(k, 1), :] = rank.astype(I32)
        base = base + jnp.sum(oh_k, axis=1, keepdims=True)
    cnt_ref[...] = jnp.broadcast_to(base, cnt_ref.shape)


def _router_kernel(n_ptiles, n_s, xp_ref, xs_ref, g_ref, shp_ref, scp_ref, shs_ref, scs_ref, wr_ref, br_ref,
                   h_ref, idx_ref, gate_ref, rank_ref, cnt_ref):
    t = pl.program_id(0)

    @pl.when(t == 0)
    def _():
        cnt_ref[...] = jnp.zeros(cnt_ref.shape, F32)

    @pl.when(t < n_ptiles)
    def _():
        h = _normmod(xp_ref[...], g_ref[...], shp_ref[...], scp_ref[...])
        h_ref[...] = h
        _route(h, ROW_TILE, wr_ref, br_ref, idx_ref, gate_ref, rank_ref, cnt_ref)

    @pl.when(t == n_ptiles)
    def _():
        hs = _normmod(xs_ref[...], g_ref[...], shs_ref[...], scs_ref[...])
        h = jnp.concatenate([hs, jnp.zeros((ROW_TILE - n_s, D_MODEL), F32)], axis=0)
        h_ref[...] = h
        _route(h, n_s, wr_ref, br_ref, idx_ref, gate_ref, rank_ref, cnt_ref)


def _router(xp, xs, gain, mods, w_router, b_router):
    n_p, n_s = xp.shape[0], xs.shape[0]
    n_pt = n_p // ROW_TILE
    n_pad = (n_pt + 1) * ROW_TILE
    seq_tiles = _seq_rows(xp) // ROW_TILE
    pt = lambda t: jnp.minimum(t, n_pt - 1)
    tok = pl.BlockSpec((TOP_K, ROW_TILE), lambda t: (0, t))
    return pl.pallas_call(
        functools.partial(_router_kernel, n_pt, n_s),
        grid=(n_pt + 1,),
        in_specs=[pl.BlockSpec((ROW_TILE, D_MODEL), lambda t: (pt(t), 0)),
                  pl.BlockSpec((n_s, D_MODEL), lambda t: (0, 0)),
                  pl.BlockSpec((1, D_MODEL), lambda t: (0, 0)),
                  mods.prompt_spec(3, D_MODEL, lambda t: pt(t) // seq_tiles),
                  mods.prompt_spec(4, D_MODEL, lambda t: pt(t) // seq_tiles),
                  mods.sample_spec(3, D_MODEL, n_s), mods.sample_spec(4, D_MODEL, n_s),
                  pl.BlockSpec((N_EXPERTS, D_MODEL), lambda t: (0, 0)),
                  pl.BlockSpec((N_EXPERTS, 1), lambda t: (0, 0))],
        out_specs=[pl.BlockSpec((ROW_TILE, D_MODEL), lambda t: (t, 0)), tok, tok, tok,
                   pl.BlockSpec((N_EXPERTS, 128), lambda t: (0, 0))],
        out_shape=[jax.ShapeDtypeStruct((n_pad, D_MODEL), F32),
                   jax.ShapeDtypeStruct((TOP_K, n_pad), I32),
                   jax.ShapeDtypeStruct((TOP_K, n_pad), F32),
                   jax.ShapeDtypeStruct((TOP_K, n_pad), I32),
                   jax.ShapeDtypeStruct((N_EXPERTS, 128), F32)],
        compiler_params=_cparams("arbitrary"),
        name="moe_router",
    )(xp, xs, gain, mods.m4, mods.m4, mods.m3, mods.m3,
      w_router.T.astype(BF16), b_router.reshape(N_EXPERTS, 1))


def _moe_expert_kernel(y_stride,
                       sb_e_ref, sb_tile0_ref, sb_ntile_ref,
                       *refs):
    del sb_e_ref, sb_tile0_ref
    dst_refs = refs[:MOE_SB_TILES]
    (h_hbm, wug_ref, wul_ref, bug_ref, bul_ref, wd_ref, bd_ref, y_hbm,
     xf_ref, xb_ref, yacc_ref, wub_ref, wdb_ref, gsem, osem) = refs[MOE_SB_TILES:]
    s = pl.program_id(0)
    f = pl.program_id(1)
    ntile = sb_ntile_ref[s]
    half = MOE_TM // 2

    def gather_copy(i, rr):
        tok = dst_refs[i][0, 0, rr] & (y_stride - 1)
        return pltpu.make_async_copy(h_hbm.at[pl.ds(tok, 1)], xf_ref.at[pl.ds(rr, 1)], gsem)

    def out_copy(i, rr):
        return pltpu.make_async_copy(yacc_ref.at[pl.ds(i * MOE_TM + rr, 1)],
                                     y_hbm.at[pl.ds(dst_refs[i][0, 0, rr], 1)], osem)

    def gather_tile_wait():
        return pltpu.make_async_copy(h_hbm.at[pl.ds(0, MOE_TM)], xf_ref, gsem)

    def out_tile_wait():
        return pltpu.make_async_copy(yacc_ref.at[pl.ds(0, MOE_TM)], y_hbm.at[pl.ds(0, MOE_TM)], osem)

    def per_row(i, make_copy):
        def body(g, c):
            for u in range(MOE_ROW_UNROLL):
                make_copy(i, g * MOE_ROW_UNROLL + u).start()
            return c
        lax.fori_loop(0, MOE_TM // MOE_ROW_UNROLL, body, 0)

    def half_chain(r0, first):
        hh = _dot(xb_ref[pl.ds(r0, half), :], wub_ref[...])
        hg = jnp.minimum(hh[:, :MOE_TF] + bug_ref[...], SWIGLU_LIMIT)
        hl = jnp.clip(hh[:, MOE_TF:] + bul_ref[...], -SWIGLU_LIMIT, SWIGLU_LIMIT)
        act = hg * (1.0 / (1.0 + jnp.exp(-SWIGLU_ALPHA * hg))) * (hl + 1.0)
        y = _dot(act.astype(BF16), wdb_ref[...])
        if first:
            yacc_ref[pl.ds(r0, half), :] = y + bd_ref[...]
        else:
            yacc_ref[pl.ds(r0, half), :] += y

    @pl.when(ntile > 0)
    def _active():
        wub_ref[:, :MOE_TF] = wug_ref[...].astype(BF16)
        wub_ref[:, MOE_TF:] = wul_ref[...].astype(BF16)
        wdb_ref[...] = wd_ref[...].astype(BF16)

        @pl.when(f == 0)
        def _first_chunk():
            per_row(0, gather_copy)
            for i in range(MOE_SB_TILES):
                @pl.when(i < ntile)
                def _():
                    gather_tile_wait().wait()
                    xb_ref[i * MOE_TM:(i + 1) * MOE_TM, :] = xf_ref[...].astype(BF16)
                    if i + 1 < MOE_SB_TILES:
                        for rr in range(MOE_TM):
                            gather_copy(i + 1, rr).start()
                    for hf in range(2):
                        half_chain(i * MOE_TM + hf * half, True)

            @pl.when(ntile < MOE_SB_TILES)
            def _():
                gather_tile_wait().wait()

        @pl.when((f > 0) & (f < MOE_NF - 1))
        def _middle_chunks():
            def tile(i, c):
                for hf in range(2):
                    half_chain(pl.multiple_of(i * MOE_TM + hf * half, half), False)
                return c
            lax.fori_loop(0, ntile, tile, 0)

        @pl.when(f == MOE_NF - 1)
        def _last_chunk():
            for i in range(MOE_SB_TILES):
                @pl.when(i < ntile)
                def _():
                    if i >= 1:
                        for rr in range(MOE_TM):
                            out_copy(i - 1, rr).start()
                    for hf in range(2):
                        half_chain(i * MOE_TM + hf * half, False)
            for i in range(MOE_SB_TILES):
                @pl.when(i == ntile - 1)
                def _():
                    per_row(i, out_copy)
            for i in range(MOE_SB_TILES):
                @pl.when(i < ntile)
                def _():
                    out_tile_wait().wait()


def _moe_n_tiles(nk):
    return (nk + N_EXPERTS * (MOE_TM - 1) + MOE_TM - 1) // MOE_TM


def _moe_n_superblocks(nk):
    return (_moe_n_tiles(nk) + N_EXPERTS * (MOE_SB_TILES - 1) + MOE_SB_TILES - 1) // MOE_SB_TILES


def _moe_plan(counts, nk):
    n_sb = _moe_n_superblocks(nk)
    tiles = (counts + MOE_TM - 1) // MOE_TM
    tile_end = jnp.cumsum(tiles)
    tile_start = tile_end - tiles
    nsb = (tiles + MOE_SB_TILES - 1) // MOE_SB_TILES
    sb_end = jnp.cumsum(nsb)
    sb_start = sb_end - nsb
    total_sb = sb_end[-1]
    s = jnp.arange(n_sb, dtype=I32)
    s_eff = jnp.minimum(s, total_sb - 1)
    e = jnp.minimum(jnp.searchsorted(sb_end, s_eff, side='right'), N_EXPERTS - 1).astype(I32)
    local = s_eff - sb_start[e]
    tile0 = tile_start[e] + local * MOE_SB_TILES
    live = s < total_sb
    ntile = jnp.where(live, jnp.clip(tiles[e] - local * MOE_SB_TILES, 0, MOE_SB_TILES), 0)
    return tile_start.astype(I32), e, tile0.astype(I32), ntile.astype(I32)


def _moe_y_stride(n_pad):
    return 1 << (n_pad - 1).bit_length()


def _moe_experts(h, idx, rank, counts, layer, w_up, b_up, w_down, b_down, n_tok):
    n_pad = h.shape[0]
    nk = n_tok * TOP_K
    tile_start, sb_e, sb_tile0, sb_ntile = _moe_plan(counts, nk)
    n_tiles = _moe_n_tiles(nk)
    n_slots = n_tiles * MOE_TM
    n_sb = sb_e.shape[0]
    assert (n_pad - n_tok) * TOP_K >= MOE_TM
    y_stride = _moe_y_stride(n_pad)
    e_ids = jnp.arange(N_EXPERTS, dtype=I32)[:, None, None]
    start_of = jnp.sum(jnp.where(idx[None, :, :n_tok] == e_ids, tile_start[:, None, None], 0), axis=0)
    slot = start_of * MOE_TM + rank[:, :n_tok]
    dst = jnp.arange(n_tok, dtype=I32)[None, :] + jnp.arange(TOP_K, dtype=I32)[:, None] * y_stride
    p = jnp.arange(n_slots, dtype=I32) % MOE_TM
    pad_dst = (p % TOP_K) * y_stride + n_tok + p // TOP_K
    slot_dst = pad_dst.at[slot.reshape(-1)].set(dst.reshape(-1)).reshape(n_tiles, 1, MOE_TM)

    def ff(s, f, nt):
        return jnp.where(nt[s] > 0, f, MOE_NF - 1)

    def dst_spec(i):
        return pl.BlockSpec((1, 1, MOE_TM), lambda s, f, e, t0, nt: (jnp.minimum(t0[s] + i, n_tiles - 1), 0, 0),
                            memory_space=pltpu.SMEM)

    grid_spec = pltpu.PrefetchScalarGridSpec(
        num_scalar_prefetch=3,
        grid=(n_sb, MOE_NF),
        in_specs=[dst_spec(i) for i in range(MOE_SB_TILES)] + [
            pl.BlockSpec(memory_space=pl.ANY),
            pl.BlockSpec((None, None, D_MODEL, MOE_TF), lambda s, f, e, t0, nt: (layer, e[s], 0, ff(s, f, nt))),
            pl.BlockSpec((None, None, D_MODEL, MOE_TF),
                         lambda s, f, e, t0, nt: (layer, e[s], 0, MOE_NF + ff(s, f, nt))),
            pl.BlockSpec((None, None, 1, MOE_TF), lambda s, f, e, t0, nt: (layer, e[s], 0, ff(s, f, nt))),
            pl.BlockSpec((None, None, 1, MOE_TF), lambda s, f, e, t0, nt: (layer, e[s], 0, MOE_NF + ff(s, f, nt))),
            pl.BlockSpec((None, None, MOE_TF, D_MODEL), lambda s, f, e, t0, nt: (layer, e[s], ff(s, f, nt), 0)),
            pl.BlockSpec((None, None, 1, D_MODEL), lambda s, f, e, t0, nt: (layer, e[s], 0, 0)),
        ],
        out_specs=pl.BlockSpec(memory_space=pl.ANY),
        scratch_shapes=[
            pltpu.VMEM((MOE_TM, D_MODEL), F32),
            pltpu.VMEM((MOE_SB_ROWS, D_MODEL), BF16),
            pltpu.VMEM((MOE_SB_ROWS, D_MODEL), F32),
            pltpu.VMEM((D_MODEL, 2 * MOE_TF), BF16),
            pltpu.VMEM((MOE_TF, D_MODEL), BF16),
            pltpu.SemaphoreType.DMA(()),
            pltpu.SemaphoreType.DMA(()),
        ],
    )
    n_l = w_up.shape[0]
    b_up4 = b_up.reshape(n_l, N_EXPERTS, 1, 2 * D_FF)
    return pl.pallas_call(
        functools.partial(_moe_expert_kernel, y_stride),
        grid_spec=grid_spec,
        out_shape=jax.ShapeDtypeStruct((TOP_K * y_stride, D_MODEL), F32),
        compiler_params=_cparams("arbitrary", "arbitrary"),
        name="moe_experts",
    )(sb_e, sb_tile0, sb_ntile, *([slot_dst] * MOE_SB_TILES), h,
      w_up, w_up, b_up4, b_up4, w_down, b_down.reshape(n_l, N_EXPERTS, 1, D_MODEL))


def _moe_combine_kernel(n_ptiles, n_s, xp_ref, xs_ref, y0_ref, y1_ref, y2_ref, y3_ref, g_ref, gp_ref, gs_ref,
                        op_ref, os_ref):
    t = pl.program_id(0)

    def mix(rows):
        acc = None
        for k, y_ref in enumerate((y0_ref, y1_ref, y2_ref, y3_ref)):
            term = g_ref[0:rows, k:k + 1] * y_ref[0:rows, :]
            acc = term if acc is None else acc + term
        return acc

    @pl.when(t < n_ptiles)
    def _():
        op_ref[...] = xp_ref[...] + gp_ref[...] * mix(ROW_TILE)

    @pl.when(t == n_ptiles)
    def _():
        os_ref[...] = xs_ref[...] + gs_ref[...] * mix(n_s)


def _moe_combine(xp, xs, y, gates_t, mods):
    n_p, n_s = xp.shape[0], xs.shape[0]
    n_pt = n_p // ROW_TILE
    seq_tiles = _seq_rows(xp) // ROW_TILE
    pt = lambda t: jnp.minimum(t, n_pt - 1)
    tiles_per_choice = y.shape[0] // TOP_K // ROW_TILE

    def yspec(k):
        return pl.BlockSpec((ROW_TILE, D_MODEL), lambda t: (k * tiles_per_choice + t, 0))

    return pl.pallas_call(
        functools.partial(_moe_combine_kernel, n_pt, n_s),
        grid=(n_pt + 1,),
        in_specs=[pl.BlockSpec((ROW_TILE, D_MODEL), lambda t: (pt(t), 0)),
                  pl.BlockSpec((n_s, D_MODEL), lambda t: (0, 0)),
                  yspec(0), yspec(1), yspec(2), yspec(3),
                  pl.BlockSpec((ROW_TILE, TOP_K), lambda t: (t, 0)),
                  mods.prompt_spec(5, D_MODEL, lambda t: pt(t) // seq_tiles),
                  mods.sample_spec(5, D_MODEL, n_s)],
        out_specs=[pl.BlockSpec((ROW_TILE, D_MODEL), lambda t: (pt(t), 0)),
                   pl.BlockSpec((n_s, D_MODEL), lambda t: (0, 0))],
        out_shape=[jax.ShapeDtypeStruct((n_p, D_MODEL), F32), jax.ShapeDtypeStruct((n_s, D_MODEL), F32)],
        compiler_params=_cparams("arbitrary"),
        name="moe_combine",
    )(xp, xs, y, y, y, y, gates_t, mods.m4, mods.m3)


def _moe_layer(xp, xs, gain, mods, layer, w_router, b_router, w_up, b_up, w_down, b_down):
    n_tok = xp.shape[0] + xs.shape[0]
    h, idx, gates, rank, cnt = _router(xp, xs, gain, mods, w_router[layer], b_router[layer])
    counts = cnt[:, 0].astype(I32)
    y = _moe_experts(h, idx, rank, counts, layer, w_up, b_up, w_down, b_down, n_tok)
    return _moe_combine(xp, xs, y, gates.T, mods)


def _qk_columns(qk_gain, n_groups):
    nq, nk = n_groups * N_HEADS, n_groups * KV_HEADS
    gain = jnp.concatenate([jnp.tile(qk_gain[0] * HEAD_DIM ** -0.5, nq), jnp.tile(qk_gain[1], nk),
                            jnp.ones((nk * HEAD_DIM,), F32)])
    flag = jnp.concatenate([jnp.ones(((nq + nk) * HEAD_DIM,), F32), jnp.zeros((nk * HEAD_DIM,), F32)])
    return gain[None, :], flag[None, :]


def _to_residue_major(x, batch):
    seq = x.shape[0] // batch
    return x.reshape(batch, seq // RES, RES, -1).transpose(0, 2, 1, 3).reshape(batch * seq, -1)


def _prompt_state(qkv_p, batch, k_off, v_off, win):
    seq = qkv_p.shape[0] // batch
    w = min(win, seq)
    r = qkv_p.reshape(batch, RES, seq // RES, -1)[:, :, (seq - w) // RES:]

    def take(off):
        a = r[..., off:off + KV_WIDTH].transpose(0, 2, 1, 3)
        return a.reshape(batch, w, KV_HEADS, HEAD_DIM)
    return jnp.stack([take(k_off), take(v_off)], axis=1)


def kernel(x_prompt, x_sample, c_prompt, c_sample, cache_a_kv, cache_b1_kv, cache_b2_kv, cache_b3_kv,
           rel_bias, norm_gain, w_ada, b_ada, w_qkv_a, qk_gain_a, sinks_a, w_o_a,
           w_qkv_b, qk_gain_b, w_o_b, w_router, b_router, w_up, b_up, w_down, b_down):
    b, s, d = x_prompt.shape
    bd, t, _ = x_sample.shape
    xp = _to_residue_major(x_prompt.reshape(b * s, d), b)
    xs = x_sample.reshape(bd * t, d)
    c_all = jnp.concatenate([c_sample, c_prompt, jnp.zeros((N_MOD_ROWS - bd - b, d), F32)], axis=0)
    mods_all = _ada(c_all, w_ada, b_ada)
    bias_p = [_prompt_bias(rel_bias, dil) for _, dil in B_PATTERNS]
    bias_d = [_decode_bias(rel_bias, dil) for _, dil in B_PATTERNS]
    caches_b = (cache_b1_kv, cache_b2_kv, cache_b3_kv)
    a_p, a_s = [], []
    b_p = [[], [], []]
    b_s = [[], [], []]
    for i in range(DEPTH):
        mods = _Mods(mods_all, i)
        j = i // 2
        hp, hs = _normmod_rows(xp, xs, norm_gain[i, 0][None, :], mods, 0, 1)
        if i % 2 == 0:
            gain, flag = _qk_columns(qk_gain_a[j], 1)
            qkv_p, qkv_s = _mm_qkv(hp, hs, w_qkv_a, j, gain, flag, 1280)
            op = _prompt_attention(qkv_p, QKV_A, 0, ATTN_WIDTH, ATTN_WIDTH + KV_WIDTH, 1, bias_p[0],
                                   sinks_a[j], F32, False)
            os_, st_s = _decode(qkv_s, [cache_a_kv], j, ((WINDOW_A, 1),), bias_d[0][None], sinks_a[j])
            a_p.append(_prompt_state(qkv_p, b, ATTN_WIDTH, ATTN_WIDTH + KV_WIDTH, WINDOW_A))
            a_s.append(st_s[0])
            w_o = w_o_a
        else:
            gain, flag = _qk_columns(qk_gain_b[j], N_GROUPS_B)
            qkv_p, qkv_s = _mm_qkv(hp, hs, w_qkv_b, j, gain, flag, 1536)
            k0 = N_GROUPS_B * ATTN_WIDTH
            v0 = k0 + N_GROUPS_B * KV_WIDTH
            outs, lses = [], []
            for gi, (win, dil) in enumerate(B_PATTERNS):
                o_g, l_g = _prompt_attention(qkv_p, QKV_B, gi * ATTN_WIDTH, k0 + gi * KV_WIDTH,
                                             v0 + gi * KV_WIDTH, dil, bias_p[gi], None, F32, True)
                outs.append(o_g)
                lses.append(l_g)
                b_p[gi].append(_prompt_state(qkv_p, b, k0 + gi * KV_WIDTH, v0 + gi * KV_WIDTH, win))
            op = _combine_groups(outs, lses)
            os_, st_s = _decode(qkv_s, caches_b, j, B_PATTERNS, jnp.stack(bias_d), None)
            for gi in range(N_GROUPS_B):
                b_s[gi].append(st_s[gi])
            w_o = w_o_b
        xp, xs = _mm_resid(op, os_, w_o, j, xp, xs, mods, 2, 1024)
        xp, xs = _moe_layer(xp, xs, norm_gain[i, 1][None, :], mods, i, w_router, b_router,
                            w_up, b_up, w_down, b_down)
    xp = xp.reshape(b, RES, s // RES, d).transpose(0, 2, 1, 3)
    return (xp.reshape(b, s, d), xs.reshape(bd, t, d),
            jnp.stack(a_p), jnp.stack(a_s),
            jnp.stack(b_p[0]), jnp.stack(b_s[0]),
            jnp.stack(b_p[1]), jnp.stack(b_s[1]),
            jnp.stack(b_p[2]), jnp.stack(b_s[2]))
```

```python
import functools
import math

import jax
import jax.numpy as jnp
import numpy as np
from jax import lax
from jax.experimental import pallas as pl
from jax.experimental.pallas import tpu as pltpu

D_MODEL = 2048
DEPTH = 4
HEAD_DIM = 64
N_HEADS = 32
KV_HEADS = 4
HEADS_PER_KV = N_HEADS // KV_HEADS
KV_WIDTH = KV_HEADS * HEAD_DIM
WINDOW_A = 128
B_PATTERNS = ((128, 1), (512, 4), (2048, 16))
N_GROUPS_B = 3
RES = 16
N_BUCKETS = 32
MAX_DISTANCE = 2048
N_EXPERTS = 32
TOP_K = 4
D_FF = 2048
SWIGLU_ALPHA = 1.702
SWIGLU_LIMIT = 7.0
ATTN_BLOCK = 128
NORM_EPS = 1e-6
NEG_INF = -1e30
ATTN_WIDTH = N_HEADS * HEAD_DIM
QKV_A = ATTN_WIDTH + 2 * KV_WIDTH
QKV_B = N_GROUPS_B * QKV_A

F32 = jnp.float32
BF16 = jnp.bfloat16
I32 = jnp.int32

VMEM_LIMIT_BYTES = 56 * 1024 * 1024
MXU_DIM = 256

N_MOD_ROWS = 40
MOD_PROMPT_ROW0 = 32
ROW_TILE = 256
MM_BM = 512
ADA_BN = 1024

MOE_TM = 256
MOE_SB_TILES = 5
MOE_SB_ROWS = MOE_TM * MOE_SB_TILES
MOE_TF = 512
MOE_NF = D_FF // MOE_TF
MOE_ROW_UNROLL = 8


def _cparams(*sem):
    return pltpu.CompilerParams(dimension_semantics=sem, vmem_limit_bytes=VMEM_LIMIT_BYTES)


def _dot(a, b):
    return jnp.dot(a, b, preferred_element_type=F32)


def _dot_nt(a, b):
    return lax.dot_general(a, b, (((1,), (1,)), ((), ())), preferred_element_type=F32)


def _ada_kernel(c_ref, w_ref, b_ref, o_ref):
    c = c_ref[...]
    a = (c / (1.0 + jnp.exp(-c))).astype(BF16)
    o_ref[...] = _dot(a, w_ref[...].astype(BF16)) + b_ref[...]


def _ada(c_all, w_ada, b_ada):
    n6 = w_ada.shape[-1]
    return pl.pallas_call(
        _ada_kernel,
        grid=(DEPTH, n6 // ADA_BN),
        in_specs=[pl.BlockSpec((N_MOD_ROWS, D_MODEL), lambda i, n: (0, 0)),
                  pl.BlockSpec((None, D_MODEL, ADA_BN), lambda i, n: (i, 0, n)),
                  pl.BlockSpec((None, 1, ADA_BN), lambda i, n: (i, 0, n))],
        out_specs=pl.BlockSpec((None, N_MOD_ROWS, ADA_BN), lambda i, n: (i, 0, n)),
        out_shape=jax.ShapeDtypeStruct((DEPTH, N_MOD_ROWS, n6), F32),
        compiler_params=_cparams("arbitrary", "arbitrary"),
        name="ada",
    )(c_all, w_ada, b_ada.reshape(DEPTH, 1, n6))


class _Mods:
    def __init__(self, mods, layer):
        self.m3 = mods
        self.m4 = mods.reshape(DEPTH, N_MOD_ROWS, 1, mods.shape[-1])
        self.layer = layer

    def prompt_spec(self, chunk, width, seq_of_step, col_of_step=None):
        i, per = self.layer, D_MODEL // width

        def imap(*g):
            col = 0 if col_of_step is None else col_of_step(*g)
            return (i, MOD_PROMPT_ROW0 + seq_of_step(*g), 0, chunk * per + col)
        return pl.BlockSpec((None, None, 1, width), imap)

    def sample_spec(self, chunk, width, n_rows, col_of_step=None):
        i, per = self.layer, D_MODEL // width

        def imap(*g):
            col = 0 if col_of_step is None else col_of_step(*g)
            return (i, 0, chunk * per + col)
        return pl.BlockSpec((None, n_rows, width), imap)


def _normmod(x, gain, shift, scale):
    y = x * lax.rsqrt(jnp.mean(x * x, axis=-1, keepdims=True) + NORM_EPS)
    return (y * gain) * (1.0 + scale) + shift


def _normmod_kernel(x_ref, g_ref, sh_ref, sc_ref, o_ref):
    o_ref[...] = _normmod(x_ref[...], g_ref[...], sh_ref[...], sc_ref[...]).astype(o_ref.dtype)


def _normmod_rows(xp, xs, gain, mods, c_shift, c_scale):
    n_p, n_s = xp.shape[0], xs.shape[0]
    seq_tiles = _seq_rows(xp) // ROW_TILE
    gspec = pl.BlockSpec((1, D_MODEL), lambda t: (0, 0))
    hp = pl.pallas_call(
        _normmod_kernel,
        grid=(n_p // ROW_TILE,),
        in_specs=[pl.BlockSpec((ROW_TILE, D_MODEL), lambda t: (t, 0)), gspec,
                  mods.prompt_spec(c_shift, D_MODEL, lambda t: t // seq_tiles),
                  mods.prompt_spec(c_scale, D_MODEL, lambda t: t // seq_tiles)],
        out_specs=pl.BlockSpec((ROW_TILE, D_MODEL), lambda t: (t, 0)),
        out_shape=jax.ShapeDtypeStruct((n_p, D_MODEL), BF16),
        compiler_params=_cparams("arbitrary"),
        name="normmod_prompt",
    )(xp, gain, mods.m4, mods.m4)
    hs = pl.pallas_call(
        _normmod_kernel,
        grid=(1,),
        in_specs=[pl.BlockSpec((n_s, D_MODEL), lambda t: (0, 0)), gspec,
                  mods.sample_spec(c_shift, D_MODEL, n_s), mods.sample_spec(c_scale, D_MODEL, n_s)],
        out_specs=pl.BlockSpec((n_s, D_MODEL), lambda t: (0, 0)),
        out_shape=jax.ShapeDtypeStruct((n_s, D_MODEL), BF16),
        compiler_params=_cparams("arbitrary"),
        name="normmod_sample",
    )(xs, gain, mods.m3, mods.m3)
    return hp, hs


_SEQ_ROWS = 2048


def _seq_rows(xp):
    return _SEQ_ROWS


def _head_sumsq(y):
    y2 = y * y
    hi = y2.astype(BF16)
    lo = (y2 - hi.astype(F32)).astype(BF16)
    r = lax.broadcasted_iota(I32, (MXU_DIM, MXU_DIM), 0) // HEAD_DIM
    c = lax.broadcasted_iota(I32, (MXU_DIM, MXU_DIM), 1) // HEAD_DIM
    seg = jnp.where(r == c, 1.0, 0.0).astype(BF16)
    outs = []
    for j in range(y.shape[1] // MXU_DIM):
        sl = slice(j * MXU_DIM, (j + 1) * MXU_DIM)
        outs.append(_dot(hi[:, sl], seg) + _dot(lo[:, sl], seg))
    return jnp.concatenate(outs, axis=1)


def _qkv_epilogue(acc, gain, flag):
    inv = lax.rsqrt(_head_sumsq(acc) * (1.0 / HEAD_DIM) + NORM_EPS)
    return jnp.where(flag > 0.5, (acc * inv) * gain, acc)


def _mm_qkv_kernel(n_ptiles, ap_ref, as_ref, w_ref, gain_ref, flag_ref, op_ref, os_ref, wb_ref):
    t = pl.program_id(1)

    @pl.when(t == 0)
    def _():
        wb_ref[...] = w_ref[...].astype(BF16)

    @pl.when(t < n_ptiles)
    def _():
        op_ref[...] = _qkv_epilogue(_dot(ap_ref[...], wb_ref[...]), gain_ref[...], flag_ref[...])

    @pl.when(t == n_ptiles)
    def _():
        os_ref[...] = _qkv_epilogue(_dot(as_ref[...], wb_ref[...]), gain_ref[...], flag_ref[...])


def _mm_qkv(hp, hs, w, layer, col_gain, col_flag, bn):
    n_p, n_s, n = hp.shape[0], hs.shape[0], w.shape[2]
    n_pt = n_p // MM_BM
    pt = lambda c, t: (jnp.minimum(t, n_pt - 1), 0)
    return pl.pallas_call(
        functools.partial(_mm_qkv_kernel, n_pt),
        grid=(n // bn, n_pt + 1),
        in_specs=[pl.BlockSpec((MM_BM, D_MODEL), pt),
                  pl.BlockSpec((n_s, D_MODEL), lambda c, t: (0, 0)),
                  pl.BlockSpec((None, D_MODEL, bn), lambda c, t: (layer, 0, c)),
                  pl.BlockSpec((1, bn), lambda c, t: (0, c)),
                  pl.BlockSpec((1, bn), lambda c, t: (0, c))],
        out_specs=[pl.BlockSpec((MM_BM, bn), lambda c, t: (jnp.minimum(t, n_pt - 1), c)),
                   pl.BlockSpec((n_s, bn), lambda c, t: (0, c))],
        out_shape=[jax.ShapeDtypeStruct((n_p, n), F32), jax.ShapeDtypeStruct((n_s, n), F32)],
        scratch_shapes=[pltpu.VMEM((D_MODEL, bn), BF16)],
        compiler_params=_cparams("arbitrary", "arbitrary"),
        name="mm_qkv",
    )(hp, hs, w, col_gain, col_flag)


def _mm_resid_kernel(n_ptiles, ap_ref, as_ref, w_ref, xp_ref, xs_ref, gp_ref, gs_ref, op_ref, os_ref, wb_ref):
    t = pl.program_id(1)

    @pl.when(t == 0)
    def _():
        wb_ref[...] = w_ref[...].astype(BF16)

    @pl.when(t < n_ptiles)
    def _():
        op_ref[...] = xp_ref[...] + gp_ref[...] * _dot(ap_ref[...].astype(BF16), wb_ref[...])

    @pl.when(t == n_ptiles)
    def _():
        os_ref[...] = xs_ref[...] + gs_ref[...] * _dot(as_ref[...], wb_ref[...])


def _mm_resid(ap, as_, w, layer, xp, xs, mods, c_gate, bn):
    n_p, n_s, k, n = ap.shape[0], as_.shape[0], w.shape[1], w.shape[2]
    n_pt = n_p // MM_BM
    seq_tiles = _seq_rows(xp) // MM_BM
    prow = lambda c, t: jnp.minimum(t, n_pt - 1)
    return pl.pallas_call(
        functools.partial(_mm_resid_kernel, n_pt),
        grid=(n // bn, n_pt + 1),
        in_specs=[pl.BlockSpec((MM_BM, k), lambda c, t: (prow(c, t), 0)),
                  pl.BlockSpec((n_s, k), lambda c, t: (0, 0)),
                  pl.BlockSpec((None, k, bn), lambda c, t: (layer, 0, c)),
                  pl.BlockSpec((MM_BM, bn), lambda c, t: (prow(c, t), c)),
                  pl.BlockSpec((n_s, bn), lambda c, t: (0, c)),
                  mods.prompt_spec(c_gate, bn, lambda c, t: prow(c, t) // seq_tiles, lambda c, t: c),
                  mods.sample_spec(c_gate, bn, n_s, lambda c, t: c)],
        out_specs=[pl.BlockSpec((MM_BM, bn), lambda c, t: (prow(c, t), c)),
                   pl.BlockSpec((n_s, bn), lambda c, t: (0, c))],
        out_shape=[jax.ShapeDtypeStruct((n_p, n), F32), jax.ShapeDtypeStruct((n_s, n), F32)],
        scratch_shapes=[pltpu.VMEM((k, bn), BF16)],
        compiler_params=_cparams("arbitrary", "arbitrary"),
        name="mm_resid",
    )(ap, as_, w, xp, xs, mods.m4, mods.m3)


def _attn_kernel(has_sink, with_lse, *refs):
    if has_sink:
        sink_ref, refs = refs[0], refs[1:]
    q_refs = refs[:KV_HEADS]
    kc_ref, kp_ref, vc_ref, vp_ref, bias_ref = refs[KV_HEADS:KV_HEADS + 5]
    o_ref = refs[KV_HEADS + 5]
    lse_ref = refs[KV_HEADS + 6] if with_lse else None

    def rows(ref):
        return ref[...].reshape(ATTN_BLOCK, ref.shape[-1])

    kw = jnp.concatenate([rows(kp_ref), rows(kc_ref)], axis=0).astype(BF16)
    vw = jnp.concatenate([rows(vp_ref), rows(vc_ref)], axis=0).astype(BF16)
    lane = lax.broadcasted_iota(I32, (ATTN_BLOCK, 128), 1)
    lse_all = jnp.zeros((ATTN_BLOCK, 128), F32)
    for kh in range(KV_HEADS):
        k_h = kw[:, kh * HEAD_DIM:(kh + 1) * HEAD_DIM]
        q_all = rows(q_refs[kh]).astype(BF16)
        outs = []
        for g in range(HEADS_PER_KV):
            h = kh * HEADS_PER_KV + g
            s = _dot_nt(q_all[:, g * HEAD_DIM:(g + 1) * HEAD_DIM], k_h) + bias_ref[h]
            m = jnp.max(s, axis=-1, keepdims=True)
            if has_sink:
                m = jnp.maximum(m, sink_ref[h])
            p = jnp.exp(s - m)
            den = jnp.sum(p, axis=-1, keepdims=True)
            if has_sink:
                den = den + jnp.exp(sink_ref[h] - m)
            o_full = _dot(p.astype(BF16), vw)
            outs.append(o_full[:, kh * HEAD_DIM:(kh + 1) * HEAD_DIM] / den)
            if with_lse:
                lse_all = jnp.where(lane == h, m + jnp.log(den), lse_all)
        lo, w = kh * HEADS_PER_KV * HEAD_DIM, HEADS_PER_KV * HEAD_DIM
        o_kh = jnp.concatenate(outs, axis=1).astype(o_ref.dtype)
        o_ref[..., lo:lo + w] = o_kh.reshape(o_ref.shape[:-1] + (w,))
    if with_lse:
        lse_ref[...] = lse_all.reshape(lse_ref.shape)


def _prompt_attention(qkv_p, width, q_off, k_off, v_off, dil, bias, sinks, out_dtype, with_lse):
    n_p = qkv_p.shape[0]
    batch = n_p // _seq_rows(qkv_p)
    m_per = _seq_rows(qkv_p) // RES
    rr = RES // 4
    nb = RES // dil
    qw = HEADS_PER_KV * HEAD_DIM
    has_sink = sinks is not None

    def view(a):
        return a.reshape(batch, rr, 4, m_per, a.shape[-1])

    def spec(cols, col_block, prev):
        def blk(j):
            return jnp.maximum(j - 1, 0) if prev else j
        if dil == 1:
            return pl.BlockSpec((None, rr, 4, ATTN_BLOCK // RES, cols), lambda b, j, r: (b, 0, 0, blk(j), col_block))
        if dil == 4:
            return pl.BlockSpec((None, rr, None, ATTN_BLOCK // rr, cols), lambda b, j, r: (b, 0, r, blk(j), col_block))
        return pl.BlockSpec((None, None, None, ATTN_BLOCK, cols), lambda b, j, r: (b, r // 4, r % 4, 0, col_block))

    in_specs = [spec(qw, q_off // qw + kh, False) for kh in range(KV_HEADS)]
    in_specs += [spec(KV_WIDTH, k_off // KV_WIDTH, False), spec(KV_WIDTH, k_off // KV_WIDTH, True),
                 spec(KV_WIDTH, v_off // KV_WIDTH, False), spec(KV_WIDTH, v_off // KV_WIDTH, True),
                 pl.BlockSpec((None, N_HEADS, ATTN_BLOCK, 2 * ATTN_BLOCK),
                              lambda b, j, r: (jnp.where(j == 0, 1, 0), 0, 0, 0))]
    args = [view(qkv_p)] * (KV_HEADS + 4) + [bias]
    if has_sink:
        in_specs = [pl.BlockSpec(memory_space=pltpu.SMEM)] + in_specs
        args = [sinks] + args
    out_specs = [spec(ATTN_WIDTH, 0, False)]
    out_shape = [jax.ShapeDtypeStruct((batch, rr, 4, m_per, ATTN_WIDTH), out_dtype)]
    if with_lse:
        out_specs.append(spec(128, 0, False))
        out_shape.append(jax.ShapeDtypeStruct((batch, rr, 4, m_per, 128), F32))
    res = pl.pallas_call(
        functools.partial(_attn_kernel, has_sink, with_lse),
        grid=(batch, nb, dil),
        in_specs=in_specs, out_specs=out_specs, out_shape=out_shape,
        compiler_params=_cparams("arbitrary", "arbitrary", "arbitrary"),
        name="prompt_attention",
    )(*args)
    o = res[0].reshape(n_p, ATTN_WIDTH)
    if with_lse:
        return o, res[1].reshape(n_p, 128)
    return o


def _block_offsets(dil):
    i = np.arange(ATTN_BLOCK)
    if dil == 1:
        return (i % (ATTN_BLOCK // RES)) * RES + i // (ATTN_BLOCK // RES)
    if dil == 4:
        per = ATTN_BLOCK // (RES // 4)
        return (i % per) * (RES // 4) + i // per
    return i


def _t5_bucket_static(dist):
    dist = np.maximum(np.asarray(dist, np.int64), 0)
    max_exact = N_BUCKETS // 2
    d = dist.astype(np.float32)
    ratio = np.log(np.maximum(d, np.float32(1.0)) / np.float32(max_exact)) / np.float32(
        math.log(MAX_DISTANCE / max_exact))
    large = max_exact + (ratio.astype(np.float32) * np.float32(N_BUCKETS - max_exact)).astype(np.int32)
    large = np.minimum(large, N_BUCKETS - 1)
    return np.where(dist < max_exact, dist, large).astype(np.int32)


def _bias_lookup(rel_bias, bucket):
    onehot = jnp.asarray(bucket[None] == np.arange(N_BUCKETS).reshape((N_BUCKETS,) + (1,) * bucket.ndim))
    tab = rel_bias.astype(F32).reshape((N_BUCKETS, N_HEADS) + (1,) * bucket.ndim)
    return jnp.sum(jnp.where(onehot[:, None], tab, 0.0), axis=0)


def _prompt_bias(rel_bias, dil):
    off = _block_offsets(dil)
    q_pos = ATTN_BLOCK + off[:, None]
    k_pos = np.concatenate([off, ATTN_BLOCK + off])[None, :]
    dist = q_pos - k_pos
    bias = _bias_lookup(rel_bias, _t5_bucket_static(dist * dil))
    valid = (dist >= 0) & (dist <= ATTN_BLOCK)
    first = valid & (k_pos >= ATTN_BLOCK)
    return jnp.stack([jnp.where(jnp.asarray(valid)[None], bias, NEG_INF),
                      jnp.where(jnp.asarray(first)[None], bias, NEG_INF)])


def _decode_bias(rel_bias, dil):
    steps = np.concatenate([ATTN_BLOCK - np.arange(ATTN_BLOCK), [0], np.zeros(ATTN_BLOCK - 1, np.int64)])
    b = _bias_lookup(rel_bias, _t5_bucket_static(dil * steps))
    return jnp.where(jnp.asarray(np.arange(2 * ATTN_BLOCK) <= ATTN_BLOCK)[None], b, 0.0)


def _bcast_heads(w):
    r = lax.broadcasted_iota(I32, (128, ATTN_WIDTH), 0)
    c = lax.broadcasted_iota(I32, (128, ATTN_WIDTH), 1) // HEAD_DIM
    e = jnp.where(r == c, 1.0, 0.0).astype(BF16)
    hi = w.astype(BF16)
    r1 = w - hi.astype(F32)
    mid = r1.astype(BF16)
    lo = (r1 - mid.astype(F32)).astype(BF16)
    return _dot(hi, e) + _dot(mid, e) + _dot(lo, e)


def _combine_kernel(o1_ref, o2_ref, o3_ref, l1_ref, l2_ref, l3_ref, o_ref):
    l1, l2, l3 = l1_ref[...], l2_ref[...], l3_ref[...]
    m = jnp.maximum(jnp.maximum(l1, l2), l3)
    e1, e2, e3 = jnp.exp(l1 - m), jnp.exp(l2 - m), jnp.exp(l3 - m)
    inv = 1.0 / (e1 + e2 + e3)
    acc = _bcast_heads(e1 * inv) * o1_ref[...]
    acc = acc + _bcast_heads(e2 * inv) * o2_ref[...]
    acc = acc + _bcast_heads(e3 * inv) * o3_ref[...]
    o_ref[...] = acc.astype(o_ref.dtype)


def _combine_groups(outs, lses):
    n_p = outs[0].shape[0]
    ospec = pl.BlockSpec((ROW_TILE, ATTN_WIDTH), lambda t: (t, 0))
    lspec = pl.BlockSpec((ROW_TILE, 128), lambda t: (t, 0))
    return pl.pallas_call(
        _combine_kernel,
        grid=(n_p // ROW_TILE,),
        in_specs=[ospec] * 3 + [lspec] * 3,
        out_specs=ospec,
        out_shape=jax.ShapeDtypeStruct((n_p, ATTN_WIDTH), BF16),
        compiler_params=_cparams("arbitrary"),
        name="combine_groups",
    )(*outs, *lses)


def _decode_kernel(groups, has_sink, has_prev, *refs):
    ng = len(groups)
    q_ref, kv_ref, bias_ref = refs[:3]
    pos = 3
    if has_sink:
        sink_ref = refs[pos]
        pos += 1
    cache_refs = refs[pos:pos + ng]
    pos += ng
    if has_prev:
        pos += ng
    state_refs = refs[pos:pos + ng]
    o_ref = refs[pos + ng]

    row_kh = lax.broadcasted_iota(I32, (N_HEADS, HEAD_DIM), 0) // HEADS_PER_KV
    outs, lses = [], []
    for g, (win, dil) in enumerate(groups):
        c_ref, st_ref = cache_refs[g], state_refs[g]
        k_new = kv_ref[:, g * KV_WIDTH:(g + 1) * KV_WIDTH]
        v_new = kv_ref[:, (ng + g) * KV_WIDTH:(ng + g + 1) * KV_WIDTH]
        for kv, new in ((0, k_new), (1, v_new)):
            st_ref[kv, pl.ds(0, 2 * win - 2), :] = c_ref[kv, pl.ds(2, 2 * win - 2), :]
            st_ref[kv, pl.ds(2 * win - 2, 1), :] = new[:, :128]
            st_ref[kv, pl.ds(2 * win - 1, 1), :] = new[:, 128:]
        k_sel = jnp.concatenate([c_ref[0, pl.ds(half, ATTN_BLOCK, stride=2 * dil), :] for half in (0, 1)], axis=1)
        v_sel = jnp.concatenate([c_ref[1, pl.ds(half, ATTN_BLOCK, stride=2 * dil), :] for half in (0, 1)], axis=1)
        q = q_ref[g]
        q_exp = jnp.concatenate([jnp.where(row_kh == kh, q, 0.0) for kh in range(KV_HEADS)], axis=1)
        qb = q_exp.astype(BF16)
        bias = bias_ref[g]
        s = _dot_nt(qb, k_sel.astype(BF16)) + bias[:, :ATTN_BLOCK]
        s0 = jnp.sum(qb.astype(F32) * k_new.astype(BF16).astype(F32), axis=-1, keepdims=True)
        s0 = s0 + bias[:, ATTN_BLOCK:ATTN_BLOCK + 1]
        m = jnp.maximum(jnp.max(s, axis=-1, keepdims=True), s0)
        if has_sink:
            m = jnp.maximum(m, sink_ref[...])
        p = jnp.exp(s - m)
        p0 = jnp.exp(s0 - m)
        den = jnp.sum(p, axis=-1, keepdims=True) + p0
        if has_sink:
            den = den + jnp.exp(sink_ref[...] - m)
        o_full = _dot(p.astype(BF16), v_sel.astype(BF16))
        o_full = o_full + p0.astype(BF16).astype(F32) * v_new.astype(BF16).astype(F32)
        o_h = jnp.zeros((N_HEADS, HEAD_DIM), F32)
        for kh in range(KV_HEADS):
            o_h = o_h + jnp.where(row_kh == kh, o_full[:, kh * HEAD_DIM:(kh + 1) * HEAD_DIM], 0.0)
        outs.append(o_h / den)
        lses.append(m + jnp.log(den))
    if ng == 1:
        o = outs[0]
    else:
        m = functools.reduce(jnp.maximum, lses)
        es = [jnp.exp(l - m) for l in lses]
        tot = functools.reduce(lambda a, b: a + b, es)
        o = functools.reduce(lambda a, b: a + b, [(e / tot) * og for e, og in zip(es, outs)])
    o_ref[...] = o.astype(o_ref.dtype)


def _decode(qkv_s, caches, layer, groups, biases, sinks, prev_states):
    n_s = qkv_s.shape[0]
    ng = len(groups)
    n_l = caches[0].shape[0]
    q = qkv_s[:, :ng * ATTN_WIDTH].reshape(n_s, ng, N_HEADS, HEAD_DIM)
    kv = qkv_s[:, ng * ATTN_WIDTH:].reshape(n_s, 1, 2 * ng * KV_WIDTH)
    has_sink = sinks is not None
    in_specs = [pl.BlockSpec((None, ng, N_HEADS, HEAD_DIM), lambda b: (b, 0, 0, 0)),
                pl.BlockSpec((None, 1, 2 * ng * KV_WIDTH), lambda b: (b, 0, 0)),
                pl.BlockSpec((ng, N_HEADS, 2 * ATTN_BLOCK), lambda b: (0, 0, 0))]
    args = [q, kv, biases]
    if has_sink:
        in_specs.append(pl.BlockSpec((N_HEADS, 1), lambda b: (0, 0)))
        args.append(sinks.reshape(N_HEADS, 1))
    out_specs, out_shape = [], []
    for (win, _), c in zip(groups, caches):
        in_specs.append(pl.BlockSpec((None, None, 2, 2 * win, 128), lambda b: (layer, b, 0, 0, 0)))
        args.append(c.reshape(c.shape[0], n_s, 2, 2 * win, 128))
        out_specs.append(pl.BlockSpec((None, None, 2, 2 * win, 128), lambda b: (layer, b, 0, 0, 0)))
        out_shape.append(jax.ShapeDtypeStruct((n_l, n_s, 2, 2 * win, 128), F32))
    aliases = {}
    if prev_states is not None:
        for g, st in enumerate(prev_states):
            aliases[len(args)] = g
            in_specs.append(pl.BlockSpec(memory_space=pl.ANY))
            args.append(st)
    out_specs.append(pl.BlockSpec((None, N_HEADS, HEAD_DIM), lambda b: (b, 0, 0)))
    out_shape.append(jax.ShapeDtypeStruct((n_s, N_HEADS, HEAD_DIM), BF16))
    res = pl.pallas_call(
        functools.partial(_decode_kernel, tuple(groups), has_sink, prev_states is not None),
        grid=(n_s,),
        in_specs=in_specs, out_specs=out_specs, out_shape=out_shape,
        input_output_aliases=aliases,
        compiler_params=_cparams("arbitrary"),
        name="decode_attention",
    )(*args)
    return res[ng].reshape(n_s, ATTN_WIDTH), list(res[:ng])


def _route(h, n_valid, wr_ref, br_ref, idx_ref, gate_ref, rank_ref, cnt_ref):
    tm = ROW_TILE
    logits = _dot_nt(wr_ref[...], h.astype(BF16)) + br_ref[...]
    e_iota = lax.broadcasted_iota(I32, (N_EXPERTS, tm), 0)
    valid = lax.broadcasted_iota(I32, (N_EXPERTS, tm), 1) < n_valid
    l = logits
    vals, sels = [], []
    for k in range(TOP_K):
        m = jnp.max(l, axis=0, keepdims=True)
        idx = jnp.min(jnp.where(l == m, e_iota, N_EXPERTS), axis=0, keepdims=True)
        sel = e_iota == idx
        idx_ref[pl.ds(k, 1), :] = idx
        vals.append(m)
        sels.append(sel & valid)
        l = jnp.where(sel, -jnp.inf, l)
    es = [jnp.exp(v - vals[0]) for v in vals]
    tot = es[0] + es[1] + es[2] + es[3]
    for k in range(TOP_K):
        gate_ref[pl.ds(k, 1), :] = es[k] / tot
    oh = jnp.concatenate([jnp.where(s, 1.0, 0.0) for s in sels], axis=0)
    r = lax.broadcasted_iota(I32, (tm, tm), 0)
    c = lax.broadcasted_iota(I32, (tm, tm), 1)
    upper = jnp.where(r < c, 1.0, 0.0).astype(BF16)
    prefix = _dot(oh.astype(BF16), upper)
    base = cnt_ref[:, 0:1]
    for k in range(TOP_K):
        oh_k = oh[k * N_EXPERTS:(k + 1) * N_EXPERTS]
        rank = jnp.sum(oh_k * (base + prefix[k * N_EXPERTS:(k + 1) * N_EXPERTS]), axis=0, keepdims=True)
        rank_ref[pl.ds(k, 1), :] = rank.astype(I32)
        base = base + jnp.sum(oh_k, axis=1, keepdims=True)
    cnt_ref[...] = jnp.broadcast_to(base, cnt_ref.shape)


def _router_kernel(n_ptiles, n_s, xp_ref, xs_ref, g_ref, shp_ref, scp_ref, shs_ref, scs_ref, wr_ref, br_ref,
                   h_ref, idx_ref, gate_ref, rank_ref, cnt_ref):
    t = pl.program_id(0)

    @pl.when(t == 0)
    def _():
        cnt_ref[...] = jnp.zeros(cnt_ref.shape, F32)

    @pl.when(t < n_ptiles)
    def _():
        h = _normmod(xp_ref[...], g_ref[...], shp_ref[...], scp_ref[...])
        h_ref[...] = h
        _route(h, ROW_TILE, wr_ref, br_ref, idx_ref, gate_ref, rank_ref, cnt_ref)

    @pl.when(t == n_ptiles)
    def _():
        hs = _normmod(xs_ref[...], g_ref[...], shs_ref[...], scs_ref[...])
        h = jnp.concatenate([hs, jnp.zeros((ROW_TILE - n_s, D_MODEL), F32)], axis=0)
        h_ref[...] = h
        _route(h, n_s, wr_ref, br_ref, idx_ref, gate_ref, rank_ref, cnt_ref)


def _router(xp, xs, gain, mods, w_router, b_router):
    n_p, n_s = xp.shape[0], xs.shape[0]
    n_pt = n_p // ROW_TILE
    n_pad = (n_pt + 1) * ROW_TILE
    seq_tiles = _seq_rows(xp) // ROW_TILE
    pt = lambda t: jnp.minimum(t, n_pt - 1)
    tok = pl.BlockSpec((TOP_K, ROW_TILE), lambda t: (0, t))
    return pl.pallas_call(
        functools.partial(_router_kernel, n_pt, n_s),
        grid=(n_pt + 1,),
        in_specs=[pl.BlockSpec((ROW_TILE, D_MODEL), lambda t: (pt(t), 0)),
                  pl.BlockSpec((n_s, D_MODEL), lambda t: (0, 0)),
                  pl.BlockSpec((1, D_MODEL), lambda t: (0, 0)),
                  mods.prompt_spec(3, D_MODEL, lambda t: pt(t) // seq_tiles),
                  mods.prompt_spec(4, D_MODEL, lambda t: pt(t) // seq_tiles),
                  mods.sample_spec(3, D_MODEL, n_s), mods.sample_spec(4, D_MODEL, n_s),
                  pl.BlockSpec((N_EXPERTS, D_MODEL), lambda t: (0, 0)),
                  pl.BlockSpec((N_EXPERTS, 1), lambda t: (0, 0))],
        out_specs=[pl.BlockSpec((ROW_TILE, D_MODEL), lambda t: (t, 0)), tok, tok, tok,
                   pl.BlockSpec((N_EXPERTS, 128), lambda t: (0, 0))],
        out_shape=[jax.ShapeDtypeStruct((n_pad, D_MODEL), F32),
                   jax.ShapeDtypeStruct((TOP_K, n_pad), I32),
                   jax.ShapeDtypeStruct((TOP_K, n_pad), F32),
                   jax.ShapeDtypeStruct((TOP_K, n_pad), I32),
                   jax.ShapeDtypeStruct((N_EXPERTS, 128), F32)],
        compiler_params=_cparams("arbitrary"),
        name="moe_router",
    )(xp, xs, gain, mods.m4, mods.m4, mods.m3, mods.m3,
      w_router.T.astype(BF16), b_router.reshape(N_EXPERTS, 1))


def _moe_expert_kernel(y_stride,
                       sb_e_ref, sb_tile0_ref, sb_ntile_ref,
                       *refs):
    del sb_e_ref, sb_tile0_ref
    dst_refs = refs[:MOE_SB_TILES]
    (h_hbm, wug_ref, wul_ref, bug_ref, bul_ref, wd_ref, bd_ref, y_hbm,
     xf_ref, xb_ref, yacc_ref, wub_ref, wdb_ref, gsem, osem) = refs[MOE_SB_TILES:]
    s = pl.program_id(0)
    f = pl.program_id(1)
    ntile = sb_ntile_ref[s]
    half = MOE_TM // 2

    def gather_copy(i, rr):
        tok = dst_refs[i][0, 0, rr] & (y_stride - 1)
        return pltpu.make_async_copy(h_hbm.at[pl.ds(tok, 1)], xf_ref.at[pl.ds(rr, 1)], gsem)

    def out_copy(i, rr):
        return pltpu.make_async_copy(yacc_ref.at[pl.ds(i * MOE_TM + rr, 1)],
                                     y_hbm.at[pl.ds(dst_refs[i][0, 0, rr], 1)], osem)

    def gather_tile_wait():
        return pltpu.make_async_copy(h_hbm.at[pl.ds(0, MOE_TM)], xf_ref, gsem)

    def out_tile_wait():
        return pltpu.make_async_copy(yacc_ref.at[pl.ds(0, MOE_TM)], y_hbm.at[pl.ds(0, MOE_TM)], osem)

    def per_row(i, make_copy):
        def body(g, c):
            for u in range(MOE_ROW_UNROLL):
                make_copy(i, g * MOE_ROW_UNROLL + u).start()
            return c
        lax.fori_loop(0, MOE_TM // MOE_ROW_UNROLL, body, 0)

    def half_chain(r0, first):
        hh = _dot(xb_ref[pl.ds(r0, half), :], wub_ref[...])
        hg = jnp.minimum(hh[:, :MOE_TF] + bug_ref[...], SWIGLU_LIMIT)
        hl = jnp.clip(hh[:, MOE_TF:] + bul_ref[...], -SWIGLU_LIMIT, SWIGLU_LIMIT)
        act = hg * (1.0 / (1.0 + jnp.exp(-SWIGLU_ALPHA * hg))) * (hl + 1.0)
        y = _dot(act.astype(BF16), wdb_ref[...])
        if first:
            yacc_ref[pl.ds(r0, half), :] = y + bd_ref[...]
        else:
            yacc_ref[pl.ds(r0, half), :] += y

    @pl.when(ntile > 0)
    def _active():
        wub_ref[:, :MOE_TF] = wug_ref[...].astype(BF16)
        wub_ref[:, MOE_TF:] = wul_ref[...].astype(BF16)
        wdb_ref[...] = wd_ref[...].astype(BF16)

        @pl.when(f == 0)
        def _first_chunk():
            per_row(0, gather_copy)
            for i in range(MOE_SB_TILES):
                @pl.when(i < ntile)
                def _():
                    gather_tile_wait().wait()
                    xb_ref[i * MOE_TM:(i + 1) * MOE_TM, :] = xf_ref[...].astype(BF16)
                    if i + 1 < MOE_SB_TILES:
                        for rr in range(MOE_TM):
                            gather_copy(i + 1, rr).start()
                    for hf in range(2):
                        half_chain(i * MOE_TM + hf * half, True)

            @pl.when(ntile < MOE_SB_TILES)
            def _():
                gather_tile_wait().wait()

        @pl.when((f > 0) & (f < MOE_NF - 1))
        def _middle_chunks():
            def tile(i, c):
                for hf in range(2):
                    half_chain(pl.multiple_of(i * MOE_TM + hf * half, half), False)
                return c
            lax.fori_loop(0, ntile, tile, 0)

        @pl.when(f == MOE_NF - 1)
        def _last_chunk():
            for i in range(MOE_SB_TILES):
                @pl.when(i < ntile)
                def _():
                    if i >= 1:
                        for rr in range(MOE_TM):
                            out_copy(i - 1, rr).start()
                    for hf in range(2):
                        half_chain(i * MOE_TM + hf * half, False)
            for i in range(MOE_SB_TILES):
                @pl.when(i == ntile - 1)
                def _():
                    per_row(i, out_copy)
            for i in range(MOE_SB_TILES):
                @pl.when(i < ntile)
                def _():
                    out_tile_wait().wait()


def _moe_n_tiles(nk):
    return (nk + N_EXPERTS * (MOE_TM - 1) + MOE_TM - 1) // MOE_TM


def _moe_n_superblocks(nk):
    return (_moe_n_tiles(nk) + N_EXPERTS * (MOE_SB_TILES - 1) + MOE_SB_TILES - 1) // MOE_SB_TILES


def _moe_plan(counts, nk):
    n_sb = _moe_n_superblocks(nk)
    tiles = (counts + MOE_TM - 1) // MOE_TM
    tile_end = jnp.cumsum(tiles)
    tile_start = tile_end - tiles
    nsb = (tiles + MOE_SB_TILES - 1) // MOE_SB_TILES
    sb_end = jnp.cumsum(nsb)
    sb_start = sb_end - nsb
    total_sb = sb_end[-1]
    s = jnp.arange(n_sb, dtype=I32)
    s_eff = jnp.minimum(s, total_sb - 1)
    e = jnp.minimum(jnp.searchsorted(sb_end, s_eff, side='right'), N_EXPERTS - 1).astype(I32)
    local = s_eff - sb_start[e]
    tile0 = tile_start[e] + local * MOE_SB_TILES
    live = s < total_sb
    ntile = jnp.where(live, jnp.clip(tiles[e] - local * MOE_SB_TILES, 0, MOE_SB_TILES), 0)
    return tile_start.astype(I32), e, tile0.astype(I32), ntile.astype(I32)


def _moe_y_stride(n_pad):
    return 1 << (n_pad - 1).bit_length()


def _moe_experts(h, idx, rank, counts, layer, w_up, b_up, w_down, b_down, n_tok):
    n_pad = h.shape[0]
    nk = n_tok * TOP_K
    tile_start, sb_e, sb_tile0, sb_ntile = _moe_plan(counts, nk)
    n_tiles = _moe_n_tiles(nk)
    n_slots = n_tiles * MOE_TM
    n_sb = sb_e.shape[0]
    assert (n_pad - n_tok) * TOP_K >= MOE_TM
    y_stride = _moe_y_stride(n_pad)
    e_ids = jnp.arange(N_EXPERTS, dtype=I32)[:, None, None]
    start_of = jnp.sum(jnp.where(idx[None, :, :n_tok] == e_ids, tile_start[:, None, None], 0), axis=0)
    slot = start_of * MOE_TM + rank[:, :n_tok]
    dst = jnp.arange(n_tok, dtype=I32)[None, :] + jnp.arange(TOP_K, dtype=I32)[:, None] * y_stride
    p = jnp.arange(n_slots, dtype=I32) % MOE_TM
    pad_dst = (p % TOP_K) * y_stride + n_tok + p // TOP_K
    slot_dst = pad_dst.at[slot.reshape(-1)].set(dst.reshape(-1)).reshape(n_tiles, 1, MOE_TM)

    def ff(s, f, nt):
        return jnp.where(nt[s] > 0, f, MOE_NF - 1)

    def dst_spec(i):
        return pl.BlockSpec((1, 1, MOE_TM), lambda s, f, e, t0, nt: (jnp.minimum(t0[s] + i, n_tiles - 1), 0, 0),
                            memory_space=pltpu.SMEM)

    grid_spec = pltpu.PrefetchScalarGridSpec(
        num_scalar_prefetch=3,
        grid=(n_sb, MOE_NF),
        in_specs=[dst_spec(i) for i in range(MOE_SB_TILES)] + [
            pl.BlockSpec(memory_space=pl.ANY),
            pl.BlockSpec((None, None, D_MODEL, MOE_TF), lambda s, f, e, t0, nt: (layer, e[s], 0, ff(s, f, nt))),
            pl.BlockSpec((None, None, D_MODEL, MOE_TF),
                         lambda s, f, e, t0, nt: (layer, e[s], 0, MOE_NF + ff(s, f, nt))),
            pl.BlockSpec((None, None, 1, MOE_TF), lambda s, f, e, t0, nt: (layer, e[s], 0, ff(s, f, nt))),
            pl.BlockSpec((None, None, 1, MOE_TF), lambda s, f, e, t0, nt: (layer, e[s], 0, MOE_NF + ff(s, f, nt))),
            pl.BlockSpec((None, None, MOE_TF, D_MODEL), lambda s, f, e, t0, nt: (layer, e[s], ff(s, f, nt), 0)),
            pl.BlockSpec((None, None, 1, D_MODEL), lambda s, f, e, t0, nt: (layer, e[s], 0, 0)),
        ],
        out_specs=pl.BlockSpec(memory_space=pl.ANY),
        scratch_shapes=[
            pltpu.VMEM((MOE_TM, D_MODEL), F32),
            pltpu.VMEM((MOE_SB_ROWS, D_MODEL), BF16),
            pltpu.VMEM((MOE_SB_ROWS, D_MODEL), F32),
            pltpu.VMEM((D_MODEL, 2 * MOE_TF), BF16),
            pltpu.VMEM((MOE_TF, D_MODEL), BF16),
            pltpu.SemaphoreType.DMA(()),
            pltpu.SemaphoreType.DMA(()),
        ],
    )
    n_l = w_up.shape[0]
    b_up4 = b_up.reshape(n_l, N_EXPERTS, 1, 2 * D_FF)
    return pl.pallas_call(
        functools.partial(_moe_expert_kernel, y_stride),
        grid_spec=grid_spec,
        out_shape=jax.ShapeDtypeStruct((TOP_K * y_stride, D_MODEL), F32),
        compiler_params=_cparams("arbitrary", "arbitrary"),
        name="moe_experts",
    )(sb_e, sb_tile0, sb_ntile, *([slot_dst] * MOE_SB_TILES), h,
      w_up, w_up, b_up4, b_up4, w_down, b_down.reshape(n_l, N_EXPERTS, 1, D_MODEL))


def _moe_combine_kernel(n_ptiles, n_s, xp_ref, xs_ref, y0_ref, y1_ref, y2_ref, y3_ref, g_ref, gp_ref, gs_ref,
                        op_ref, os_ref):
    t = pl.program_id(0)

    def mix(rows):
        acc = None
        for k, y_ref in enumerate((y0_ref, y1_ref, y2_ref, y3_ref)):
            term = g_ref[0:rows, k:k + 1] * y_ref[0:rows, :]
            acc = term if acc is None else acc + term
        return acc

    @pl.when(t < n_ptiles)
    def _():
        op_ref[...] = xp_ref[...] + gp_ref[...] * mix(ROW_TILE)

    @pl.when(t == n_ptiles)
    def _():
        os_ref[...] = xs_ref[...] + gs_ref[...] * mix(n_s)


def _moe_combine(xp, xs, y, gates_t, mods):
    n_p, n_s = xp.shape[0], xs.shape[0]
    n_pt = n_p // ROW_TILE
    seq_tiles = _seq_rows(xp) // ROW_TILE
    pt = lambda t: jnp.minimum(t, n_pt - 1)
    tiles_per_choice = y.shape[0] // TOP_K // ROW_TILE

    def yspec(k):
        return pl.BlockSpec((ROW_TILE, D_MODEL), lambda t: (k * tiles_per_choice + t, 0))

    return pl.pallas_call(
        functools.partial(_moe_combine_kernel, n_pt, n_s),
        grid=(n_pt + 1,),
        in_specs=[pl.BlockSpec((ROW_TILE, D_MODEL), lambda t: (pt(t), 0)),
                  pl.BlockSpec((n_s, D_MODEL), lambda t: (0, 0)),
                  yspec(0), yspec(1), yspec(2), yspec(3),
                  pl.BlockSpec((ROW_TILE, TOP_K), lambda t: (t, 0)),
                  mods.prompt_spec(5, D_MODEL, lambda t: pt(t) // seq_tiles),
                  mods.sample_spec(5, D_MODEL, n_s)],
        out_specs=[pl.BlockSpec((ROW_TILE, D_MODEL), lambda t: (pt(t), 0)),
                   pl.BlockSpec((n_s, D_MODEL), lambda t: (0, 0))],
        out_shape=[jax.ShapeDtypeStruct((n_p, D_MODEL), F32), jax.ShapeDtypeStruct((n_s, D_MODEL), F32)],
        compiler_params=_cparams("arbitrary"),
        name="moe_combine",
    )(xp, xs, y, y, y, y, gates_t, mods.m4, mods.m3)


def _moe_layer(xp, xs, gain, mods, layer, w_router, b_router, w_up, b_up, w_down, b_down):
    n_tok = xp.shape[0] + xs.shape[0]
    h, idx, gates, rank, cnt = _router(xp, xs, gain, mods, w_router[layer], b_router[layer])
    counts = cnt[:, 0].astype(I32)
    y = _moe_experts(h, idx, rank, counts, layer, w_up, b_up, w_down, b_down, n_tok)
    return _moe_combine(xp, xs, y, gates.T, mods)


def _qk_columns(qk_gain, n_groups):
    nq, nk = n_groups * N_HEADS, n_groups * KV_HEADS
    gain = jnp.concatenate([jnp.tile(qk_gain[0] * HEAD_DIM ** -0.5, nq), jnp.tile(qk_gain[1], nk),
                            jnp.ones((nk * HEAD_DIM,), F32)])
    flag = jnp.concatenate([jnp.ones(((nq + nk) * HEAD_DIM,), F32), jnp.zeros((nk * HEAD_DIM,), F32)])
    return gain[None, :], flag[None, :]


def _to_residue_major(x, batch):
    seq = x.shape[0] // batch
    return x.reshape(batch, seq // RES, RES, -1).transpose(0, 2, 1, 3).reshape(batch * seq, -1)


def _prompt_state(qkv_p, batch, k_off, v_off, win):
    seq = qkv_p.shape[0] // batch
    w = min(win, seq)
    r = qkv_p.reshape(batch, RES, seq // RES, -1)[:, :, (seq - w) // RES:]

    def take(off):
        a = r[..., off:off + KV_WIDTH].transpose(0, 2, 1, 3)
        return a.reshape(batch, w, KV_HEADS, HEAD_DIM)
    return jnp.stack([take(k_off), take(v_off)], axis=1)


def kernel(x_prompt, x_sample, c_prompt, c_sample, cache_a_kv, cache_b1_kv, cache_b2_kv, cache_b3_kv,
           rel_bias, norm_gain, w_ada, b_ada, w_qkv_a, qk_gain_a, sinks_a, w_o_a,
           w_qkv_b, qk_gain_b, w_o_b, w_router, b_router, w_up, b_up, w_down, b_down):
    b, s, d = x_prompt.shape
    bd, t, _ = x_sample.shape
    xp = _to_residue_major(x_prompt.reshape(b * s, d), b)
    xs = x_sample.reshape(bd * t, d)
    c_all = jnp.concatenate([c_sample, c_prompt, jnp.zeros((N_MOD_ROWS - bd - b, d), F32)], axis=0)
    mods_all = _ada(c_all, w_ada, b_ada)
    bias_p = [_prompt_bias(rel_bias, dil) for _, dil in B_PATTERNS]
    bias_d = [_decode_bias(rel_bias, dil) for _, dil in B_PATTERNS]
    caches_b = (cache_b1_kv, cache_b2_kv, cache_b3_kv)
    a_p, a_s = [], None
    b_p = [[], [], []]
    b_s = None
    for i in range(DEPTH):
        mods = _Mods(mods_all, i)
        j = i // 2
        hp, hs = _normmod_rows(xp, xs, norm_gain[i, 0][None, :], mods, 0, 1)
        if i % 2 == 0:
            gain, flag = _qk_columns(qk_gain_a[j], 1)
            qkv_p, qkv_s = _mm_qkv(hp, hs, w_qkv_a, j, gain, flag, 1280)
            op = _prompt_attention(qkv_p, QKV_A, 0, ATTN_WIDTH, ATTN_WIDTH + KV_WIDTH, 1, bias_p[0],
                                   sinks_a[j], F32, False)
            os_, a_s = _decode(qkv_s, [cache_a_kv], j, ((WINDOW_A, 1),), bias_d[0][None], sinks_a[j], a_s)
            a_p.append(_prompt_state(qkv_p, b, ATTN_WIDTH, ATTN_WIDTH + KV_WIDTH, WINDOW_A))
            w_o = w_o_a
        else:
            gain, flag = _qk_columns(qk_gain_b[j], N_GROUPS_B)
            qkv_p, qkv_s = _mm_qkv(hp, hs, w_qkv_b, j, gain, flag, 1536)
            k0 = N_GROUPS_B * ATTN_WIDTH
            v0 = k0 + N_GROUPS_B * KV_WIDTH
            outs, lses = [], []
            for gi, (win, dil) in enumerate(B_PATTERNS):
                o_g, l_g = _prompt_attention(qkv_p, QKV_B, gi * ATTN_WIDTH, k0 + gi * KV_WIDTH,
                                             v0 + gi * KV_WIDTH, dil, bias_p[gi], None, F32, True)
                outs.append(o_g)
                lses.append(l_g)
                b_p[gi].append(_prompt_state(qkv_p, b, k0 + gi * KV_WIDTH, v0 + gi * KV_WIDTH, win))
            op = _combine_groups(outs, lses)
            os_, b_s = _decode(qkv_s, caches_b, j, B_PATTERNS, jnp.stack(bias_d), None, b_s)
            w_o = w_o_b
        xp, xs = _mm_resid(op, os_, w_o, j, xp, xs, mods, 2, 1024)
        xp, xs = _moe_layer(xp, xs, norm_gain[i, 1][None, :], mods, i, w_router, b_router,
                            w_up, b_up, w_down, b_down)
    xp = xp.reshape(b, RES, s // RES, d).transpose(0, 2, 1, 3)
    def sample_state(st, win):
        return st.reshape(st.shape[0], bd, 2, win, KV_HEADS, HEAD_DIM)

    return (xp.reshape(b, s, d), xs.reshape(bd, t, d),
            jnp.stack(a_p), sample_state(a_s[0], WINDOW_A),
            jnp.stack(b_p[0]), sample_state(b_s[0], B_PATTERNS[0][0]),
            jnp.stack(b_p[1]), sample_state(b_s[1], B_PATTERNS[1][0]),
            jnp.stack(b_p[2]), sample_state(b_s[2], B_PATTERNS[2][0]))
```
